```python
import math
import jax, jax.numpy as jnp
from jax import lax
import numpy as np

D_MODEL = 2048
BATCH = 2
SEQ = 4096
DEPTH = 1

PLE_DIM = 256
D_FF = ((8 * D_MODEL // 3 + 255) // 256) * 256
MIX_WIDTH = D_MODEL
GDN_HEAD_DIM = 128
GDN_WIDTH = MIX_WIDTH // 2
GDN_HEADS = GDN_WIDTH // GDN_HEAD_DIM
SSM_HEAD_DIM = 64
SSM_WIDTH = MIX_WIDTH - GDN_WIDTH
SSM_HEADS = SSM_WIDTH // SSM_HEAD_DIM
SSM_GROUPS = 2
SSM_STATE = 128
CONV_K = 4
CHUNK = 64
EPS = 1e-6
GDN_QKV_DIM = 3 * GDN_WIDTH
SSM_XBC_DIM = SSM_WIDTH + 2 * SSM_GROUPS * SSM_STATE
IN_SIZES = (GDN_QKV_DIM, GDN_WIDTH, GDN_HEADS, GDN_HEADS, SSM_WIDTH, SSM_XBC_DIM, SSM_HEADS)
IN_DIM = sum(IN_SIZES)
IN_SPLITS = tuple(int(s) for s in np.cumsum(IN_SIZES)[:-1])

kernel_name = 'hybrid_gdn_ssd_macaron_block'


def rmsnorm(x, w):
    xf = x.astype(jnp.float32)
    y = xf * lax.rsqrt(jnp.mean(xf * xf, axis=-1, keepdims=True) + EPS)
    return (y * w.astype(jnp.float32)).astype(x.dtype)


def l2norm(x):
    return x * lax.rsqrt(jnp.sum(x * x, axis=-1, keepdims=True) + EPS)


def swiglu(x, w_gate, w_up, w_down):
    return (jax.nn.silu(x @ w_gate) * (x @ w_up)) @ w_down


def causal_depthwise_conv(x, w):
    return lax.conv_general_dilated(
        x, w[:, None, :].astype(x.dtype), window_strides=(1,), padding=[(CONV_K - 1, 0)],
        dimension_numbers=('NWC', 'WIO', 'NWC'), feature_group_count=x.shape[-1])


def gated_delta_rule_chunked(q, k, v, g, beta):
    b, t, h, dk = q.shape
    dv = v.shape[-1]
    n = t // CHUNK

    def chunks(a):
        return jnp.swapaxes(a.reshape(b, n, CHUNK, h, *a.shape[3:]), 2, 3)

    q, k, v, g, beta = (chunks(a) for a in (q, k, v, g, beta))
    causal = jnp.tril(jnp.ones((CHUNK, CHUNK), dtype=bool))
    strict = jnp.tril(jnp.ones((CHUNK, CHUNK), dtype=bool), k=-1)
    gc = jnp.cumsum(g, axis=-1)
    decay = jnp.exp(jnp.where(causal, gc[..., :, None] - gc[..., None, :], -jnp.inf))
    kb = k * beta[..., None]
    lower = jnp.where(strict, jnp.einsum('bnhik,bnhjk->bnhij', kb, k) * decay, 0.0)
    eye = jnp.eye(CHUNK, dtype=q.dtype)
    t_inv = lax.linalg.triangular_solve(eye + lower, jnp.broadcast_to(eye, lower.shape),
                                        left_side=True, lower=True)
    u = jnp.einsum('bnhij,bnhjv->bnhiv', t_inv, v * beta[..., None])
    w = jnp.einsum('bnhij,bnhjk->bnhik', t_inv, kb * jnp.exp(gc)[..., None])
    qk = jnp.einsum('bnhik,bnhjk->bnhij', q, k) * decay
    q_dec = q * jnp.exp(gc)[..., None]
    k_dec = k * jnp.exp(gc[..., -1:] - gc)[..., None]
    g_tot = jnp.exp(gc[..., -1])

    def step(state, inp):
        u_c, w_c, qk_c, qd_c, kd_c, gt_c = inp
        v_new = u_c - jnp.einsum('bhck,bhkv->bhcv', w_c, state)
        o_c = (jnp.einsum('bhck,bhkv->bhcv', qd_c, state)
               + jnp.einsum('bhij,bhjv->bhiv', qk_c, v_new))
        state = state * gt_c[..., None, None] + jnp.einsum('bhck,bhcv->bhkv', kd_c, v_new)
        return state, o_c

    xs = tuple(jnp.moveaxis(a, 1, 0) for a in (u, w, qk, q_dec, k_dec, g_tot))
    s0 = jnp.zeros((b, h, dk, dv), q.dtype)
    _, o = lax.scan(step, s0, xs)
    return jnp.transpose(o, (1, 0, 3, 2, 4)).reshape(b, t, h, dv)


def ssd_chunked(x, dt, a_neg, bm, cm):
    b, t, nh, hp = x.shape
    ng, ns = bm.shape[2], bm.shape[3]
    r = nh // ng
    n = t // CHUNK
    xc = (x * dt[..., None]).reshape(b, n, CHUNK, ng, r, hp)
    la = jnp.moveaxis((dt * a_neg).reshape(b, n, CHUNK, ng, r), 2, -1)
    bc = bm.reshape(b, n, CHUNK, ng, ns)
    cc = cm.reshape(b, n, CHUNK, ng, ns)
    acs = jnp.cumsum(la, axis=-1)
    causal = jnp.tril(jnp.ones((CHUNK, CHUNK), dtype=bool))
    seg = jnp.exp(jnp.where(causal, acs[..., :, None] - acs[..., None, :], -jnp.inf))
    cb = jnp.einsum('bnlgd,bnsgd->bngls', cc, bc)
    y_diag = jnp.einsum('bngrls,bnsgrp->bnlgrp', cb[:, :, :, None] * seg, xc)
    decay_to_end = jnp.exp(acs[..., -1:] - acs)
    chunk_states = jnp.einsum('bnsgd,bngrs,bnsgrp->bngrpd', bc, decay_to_end, xc)
    chunk_decay = jnp.exp(acs[..., -1])

    def step(state, inp):
        cs, cd = inp
        return state * cd[..., None, None] + cs, state

    s0 = jnp.zeros((b, ng, r, hp, ns), x.dtype)
    _, prev = lax.scan(step, s0, (jnp.moveaxis(chunk_states, 1, 0), jnp.moveaxis(chunk_decay, 1, 0)))
    prev = jnp.moveaxis(prev, 0, 1)
    y_off = jnp.einsum('bnlgd,bngrpd,bngrl->bnlgrp', cc, prev, jnp.exp(acs))
    return (y_diag + y_off).reshape(b, t, nh, hp)


def gdn_mixer(qkv, gate, a, bl, conv_w, a_log, dt_bias, norm_w):
    b, t, _ = qkv.shape
    qkv = jax.nn.silu(causal_depthwise_conv(qkv, conv_w))
    q, k, v = jnp.split(qkv.astype(jnp.float32), 3, axis=-1)
    q = l2norm(q.reshape(b, t, GDN_HEADS, GDN_HEAD_DIM)) * (GDN_HEAD_DIM ** -0.5)
    k = l2norm(k.reshape(b, t, GDN_HEADS, GDN_HEAD_DIM))
    v = v.reshape(b, t, GDN_HEADS, GDN_HEAD_DIM)
    beta = jax.nn.sigmoid(bl.astype(jnp.float32))
    g = -jnp.exp(a_log.astype(jnp.float32)) * jax.nn.softplus(
        a.astype(jnp.float32) + dt_bias.astype(jnp.float32))
    o = gated_delta_rule_chunked(q, k, v, g, beta)
    o = rmsnorm(o, norm_w) * jax.nn.silu(
        gate.astype(jnp.float32).reshape(b, t, GDN_HEADS, GDN_HEAD_DIM))
    return o.reshape(b, t, GDN_WIDTH).astype(qkv.dtype)


def ssd_mixer(z, xbc, dt_raw, conv_w, conv_b, a_log, dt_bias, d_skip, norm_w):
    b, t, _ = xbc.shape
    xbc = jax.nn.silu(causal_depthwise_conv(xbc, conv_w) + conv_b.astype(xbc.dtype))
    xs, bm, cm = jnp.split(xbc.astype(jnp.float32),
                           [SSM_WIDTH, SSM_WIDTH + SSM_GROUPS * SSM_STATE], axis=-1)
    xs = xs.reshape(b, t, SSM_HEADS, SSM_HEAD_DIM)
    bm = bm.reshape(b, t, SSM_GROUPS, SSM_STATE)
    cm = cm.reshape(b, t, SSM_GROUPS, SSM_STATE)
    dt = jax.nn.softplus(dt_raw.astype(jnp.float32) + dt_bias.astype(jnp.float32))
    a_neg = -jnp.exp(a_log.astype(jnp.float32))
    y = ssd_chunked(xs, dt, a_neg, bm, cm) + xs * d_skip.astype(jnp.float32)[:, None]
    y = y.reshape(b, t, SSM_WIDTH) * jax.nn.silu(z.astype(jnp.float32))
    gsz = SSM_WIDTH // SSM_GROUPS
    y = rmsnorm(y.reshape(b, t, SSM_GROUPS, gsz), norm_w.reshape(SSM_GROUPS, gsz))
    return y.reshape(b, t, SSM_WIDTH).astype(xbc.dtype)


def setup_inputs(seed: int = 0) -> dict:
    key = jax.random.key(seed)
    ks = jax.random.split(key, 28)

    def normal(k, shape, scale):
        return jax.random.normal(k, shape, jnp.float32) * scale

    def gain(k, shape):
        return 1.0 + 0.02 * jax.random.normal(k, shape, jnp.float32)

    def dt_bias(k, shape):
        dt = jnp.exp(jax.random.uniform(k, shape, jnp.float32, math.log(1e-3), math.log(1e-1)))
        return dt + jnp.log(-jnp.expm1(-dt))

    def a_log(k, shape):
        return jnp.log(jax.random.uniform(k, shape, jnp.float32, 1.0, 16.0))

    def conv_w(k, shape):
        bound = CONV_K ** -0.5
        return jax.random.uniform(k, shape, jnp.float32, -bound, bound)

    L, D, F = DEPTH, D_MODEL, D_FF
    return {
        'x': normal(ks[0], (BATCH, SEQ, D), 1.0),
        'p': normal(ks[1], (L, BATCH, SEQ, PLE_DIM), 1.0),
        'ffn1_norm': gain(ks[2], (L, D)),
        'ffn1_w_gate': normal(ks[3], (L, D, F), D ** -0.5),
        'ffn1_w_up': normal(ks[4], (L, D, F), D ** -0.5),
        'ffn1_w_down': normal(ks[5], (L, F, D), F ** -0.5),
        'mix_norm': gain(ks[6], (L, D)),
        'w_in': normal(ks[7], (L, D, IN_DIM), D ** -0.5),
        'gdn_conv_w': conv_w(ks[8], (L, CONV_K, GDN_QKV_DIM)),
        'gdn_a_log': a_log(ks[9], (L, GDN_HEADS)),
        'gdn_dt_bias': dt_bias(ks[10], (L, GDN_HEADS)),
        'gdn_out_norm': gain(ks[11], (L, GDN_HEAD_DIM)),
        'ssm_conv_w': conv_w(ks[12], (L, CONV_K, SSM_XBC_DIM)),
        'ssm_conv_b': normal(ks[13], (L, SSM_XBC_DIM), 0.02),
        'ssm_a_log': a_log(ks[14], (L, SSM_HEADS)),
        'ssm_dt_bias': dt_bias(ks[15], (L, SSM_HEADS)),
        'ssm_d': gain(ks[16], (L, SSM_HEADS)),
        'ssm_out_norm': gain(ks[17], (L, SSM_WIDTH)),
        'w_out': normal(ks[18], (L, MIX_WIDTH, D), MIX_WIDTH ** -0.5),
        'ffn2_norm': gain(ks[19], (L, D)),
        'ffn2_w_gate': normal(ks[20], (L, D, F), D ** -0.5),
        'ffn2_w_up': normal(ks[21], (L, D, F), D ** -0.5),
        'ffn2_w_down': normal(ks[22], (L, F, D), F ** -0.5),
        'ple_norm': gain(ks[23], (L, D)),
        'ple_w_gate': normal(ks[24], (L, D, D), D ** -0.5),
        'ple_w_proj': normal(ks[25], (L, PLE_DIM, D), PLE_DIM ** -0.5),
        'ple_post_norm': gain(ks[26], (L, D)),
        'final_norm': gain(ks[27], (D,)),
    }


def reference(x, p, ffn1_norm, ffn1_w_gate, ffn1_w_up, ffn1_w_down, mix_norm, w_in,
              gdn_conv_w, gdn_a_log, gdn_dt_bias, gdn_out_norm,
              ssm_conv_w, ssm_conv_b, ssm_a_log, ssm_dt_bias, ssm_d, ssm_out_norm,
              w_out, ffn2_norm, ffn2_w_gate, ffn2_w_up, ffn2_w_down,
              ple_norm, ple_w_gate, ple_w_proj, ple_post_norm, final_norm):
    h = x
    for i in range(DEPTH):
        h = h + 0.5 * swiglu(rmsnorm(h, ffn1_norm[i]), ffn1_w_gate[i], ffn1_w_up[i], ffn1_w_down[i])
        proj = rmsnorm(h, mix_norm[i]) @ w_in[i]
        g_qkv, g_gate, g_a, g_b, s_z, s_xbc, s_dt = jnp.split(proj, IN_SPLITS, axis=-1)
        o_gdn = gdn_mixer(g_qkv, g_gate, g_a, g_b, gdn_conv_w[i], gdn_a_log[i],
                          gdn_dt_bias[i], gdn_out_norm[i])
        o_ssm = ssd_mixer(s_z, s_xbc, s_dt, ssm_conv_w[i], ssm_conv_b[i], ssm_a_log[i],
                          ssm_dt_bias[i], ssm_d[i], ssm_out_norm[i])
        h = h + jnp.concatenate([o_gdn, o_ssm], axis=-1) @ w_out[i]
        h = h + 0.5 * swiglu(rmsnorm(h, ffn2_norm[i]), ffn2_w_gate[i], ffn2_w_up[i], ffn2_w_down[i])
        gate = jax.nn.sigmoid(rmsnorm(h, ple_norm[i]) @ ple_w_gate[i])
        h = h + gate * rmsnorm(p[i] @ ple_w_proj[i], ple_post_norm[i])
    return rmsnorm(h, final_norm)
```

```python
import functools

import jax
import jax.numpy as jnp
from jax import lax
from jax.experimental import pallas as pl
from jax.experimental.pallas import tpu as pltpu

F32 = jnp.float32
BF16 = jnp.bfloat16
EPS = 1e-6

LANE = 128
CONV_K = 4
CONV_HALO = 8
MIX_CHUNK = 128
GDN_HEADS = 8
GDN_HEAD_DIM = 128
GDN_WIDTH = GDN_HEADS * GDN_HEAD_DIM
SSM_HEADS = 16
SSM_HEAD_DIM = 64
SSM_WIDTH = SSM_HEADS * SSM_HEAD_DIM
SSM_GROUPS = 2
SSM_STATE = 128
SSM_GROUP_WIDTH = SSM_WIDTH // SSM_GROUPS
VMEM_LIMIT = 56 * 1024 * 1024


def _cparams(sem):
    return pltpu.CompilerParams(dimension_semantics=sem, vmem_limit_bytes=VMEM_LIMIT)


def _dot(a, b):
    return jnp.dot(a, b, preferred_element_type=F32)


def _dot_nt(a, b):
    return lax.dot_general(a, b, (((1,), (1,)), ((), ())), preferred_element_type=F32)


def _bdot(a, b):
    return _dot(a.astype(BF16), b.astype(BF16))


def _split(a):
    hi = a.astype(BF16)
    lo = (a - hi.astype(F32)).astype(BF16)
    return hi, lo


def _dot3(a, b):
    ah, al = _split(a)
    bh, bl = _split(b)
    return _dot(ah, bh) + (_dot(al, bh) + _dot(ah, bl))


def _dot_f32(a, b):
    return jnp.dot(a, b, preferred_element_type=F32, precision=lax.Precision.HIGHEST)


def _sigmoid(x):
    return 1.0 / (1.0 + jnp.exp(-x))


def _silu(x):
    return x * _sigmoid(x)


def _softplus(x):
    return jnp.maximum(x, 0.0) + jnp.log1p(jnp.exp(-jnp.abs(x)))


def _rms(x):
    return x * lax.rsqrt(jnp.mean(x * x, axis=-1, keepdims=True) + EPS)


def _ffn_kernel(x_ref, nw_ref, wg_ref, wu_ref, wd_ref, o_ref, xn_ref, acc_ref):
    j = pl.program_id(1)

    @pl.when(j == 0)
    def _():
        xn_ref[...] = (_rms(x_ref[...]) * nw_ref[...]).astype(BF16)
        acc_ref[...] = jnp.zeros_like(acc_ref)

    xn = xn_ref[...]
    g = _dot(xn, wg_ref[...])
    u = _dot(xn, wu_ref[...])
    h = (_silu(g) * u).astype(BF16)
    acc_ref[...] += _dot(h, wd_ref[...])

    @pl.when(j == pl.num_programs(1) - 1)
    def _():
        o_ref[...] = x_ref[...] + 0.5 * acc_ref[...]


def _ffn(x, norm_w, wg, wu, wd, *, tm=512, tf=512):
    n, d = x.shape
    f = wg.shape[1]
    return pl.pallas_call(
        _ffn_kernel,
        grid=(n // tm, f // tf),
        in_specs=[
            pl.BlockSpec((tm, d), lambda i, j: (i, 0)),
            pl.BlockSpec((1, d), lambda i, j: (0, 0)),
            pl.BlockSpec((d, tf), lambda i, j: (0, j)),
            pl.BlockSpec((d, tf), lambda i, j: (0, j)),
            pl.BlockSpec((tf, d), lambda i, j: (j, 0)),
        ],
        out_specs=pl.BlockSpec((tm, d), lambda i, j: (i, 0)),
        out_shape=jax.ShapeDtypeStruct((n, d), F32),
        scratch_shapes=[pltpu.VMEM((tm, d), BF16), pltpu.VMEM((tm, d), F32)],
        compiler_params=_cparams(("parallel", "arbitrary")),
        name="ffn",
    )(x, norm_w.reshape(1, d), wg, wu, wd)


def _norm_matmul_kernel(x_ref, nw_ref, w_ref, o_ref):
    xn = (_rms(x_ref[...]) * nw_ref[...]).astype(BF16)
    o_ref[...] = _dot(xn, w_ref[...])


def _norm_matmul(x, norm_w, w, *, tm=512, tn):
    n, d = x.shape
    cols = w.shape[1]
    return pl.pallas_call(
        _norm_matmul_kernel,
        grid=(cols // tn, n // tm),
        in_specs=[
            pl.BlockSpec((tm, d), lambda j, i: (i, 0)),
            pl.BlockSpec((1, d), lambda j, i: (0, 0)),
            pl.BlockSpec((d, tn), lambda j, i: (0, j)),
        ],
        out_specs=pl.BlockSpec((tm, tn), lambda j, i: (i, j)),
        out_shape=jax.ShapeDtypeStruct((n, cols), F32),
        compiler_params=_cparams(("parallel", "parallel")),
        name="in_proj",
    )(x, norm_w.reshape(1, d), w)


def _causal_conv(cur_ref, prev_ref, w_ref, lo, width, use_prev):
    c = cur_ref.shape[0]
    xx = jnp.concatenate([prev_ref[:, lo:lo + width] * use_prev, cur_ref[:, lo:lo + width]], axis=0)
    y = w_ref[CONV_K - 1:CONV_K, lo:lo + width] * xx[CONV_HALO:CONV_HALO + c]
    for j in range(CONV_K - 1):
        shifted = pltpu.roll(xx, CONV_K - 1 - j, axis=0)
        y = y + w_ref[j:j + 1, lo:lo + width] * shifted[CONV_HALO:CONV_HALO + c]
    return y


def _inv_unit_lower(a, row, col):
    c = a.shape[0]
    base = 16
    x = jnp.where((row // base) == (col // base), -a, 0.0)
    t = jnp.where(row == col, 1.0, 0.0) + x
    p = x
    for _ in range(3):
        p = _dot3(p, p)
        t = t + _dot3(t, p)
    b = 2 * base
    while b <= c:
        e = jnp.where(((row // b) == (col // b)) & ((row // (b // 2)) != (col // (b // 2))), a, 0.0)
        t = t - _dot3(t, _dot3(e, t))
        b *= 2
    return t


def _gdn_kernel(cur_ref, prev_ref, cw_ref, alog_ref, dtb_ref, onw_ref, o_ref, s_ref):
    t_idx = pl.program_id(1)

    @pl.when(t_idx == 0)
    def _():
        s_ref[...] = jnp.zeros_like(s_ref)

    c = cur_ref.shape[0]
    hd = GDN_HEAD_DIM
    use_prev = jnp.where(t_idx > 0, 1.0, 0.0)
    row = lax.broadcasted_iota(jnp.int32, (c, c), 0)
    col = lax.broadcasted_iota(jnp.int32, (c, c), 1)
    causal = row >= col
    strict = row > col
    tril = jnp.where(causal, 1.0, 0.0)

    small = cur_ref[:, 4 * GDN_WIDTH:4 * GDN_WIDTH + LANE]
    g = -jnp.exp(alog_ref[...]) * _softplus(small + dtb_ref[...])
    beta = _sigmoid(small)
    gc = _dot_f32(tril, g)
    gc_t = gc.T
    g_last = gc[c - 1:c, :]
    e_gc = jnp.exp(gc)
    e_rest = jnp.exp(g_last - gc)
    g_tot = jnp.exp(g_last)

    for h in range(GDN_HEADS):
        qh = _silu(_causal_conv(cur_ref, prev_ref, cw_ref, h * hd, hd, use_prev))
        kh = _silu(_causal_conv(cur_ref, prev_ref, cw_ref, GDN_WIDTH + h * hd, hd, use_prev))
        vh = _silu(_causal_conv(cur_ref, prev_ref, cw_ref, 2 * GDN_WIDTH + h * hd, hd, use_prev))
        qh = qh * lax.rsqrt(jnp.sum(qh * qh, axis=-1, keepdims=True) + EPS) * (hd ** -0.5)
        kh = kh * lax.rsqrt(jnp.sum(kh * kh, axis=-1, keepdims=True) + EPS)
        beta_c = beta[:, GDN_HEADS + h:GDN_HEADS + h + 1]
        gc_c = gc[:, h:h + 1]
        gc_r = gc_t[h:h + 1, :]
        decay = jnp.exp(jnp.where(causal, gc_c - gc_r, -jnp.inf))
        kb = kh * beta_c
        kh_b = kh.astype(BF16)
        a = jnp.where(strict, _dot_nt(kb.astype(BF16), kh_b) * decay, 0.0)
        t_inv = _inv_unit_lower(a, row, col)
        rhs = jnp.concatenate([vh * beta_c, kb * e_gc[:, h:h + 1]], axis=1)
        uw = _bdot(t_inv, rhs)
        u, w = uw[:, :hd], uw[:, hd:]
        qk = _dot_nt(qh.astype(BF16), kh_b) * decay
        q_dec = qh * e_gc[:, h:h + 1]
        k_dec = kh * e_rest[:, h:h + 1]
        s = s_ref[h]
        s_b = s.astype(BF16)
        v_new = u - _dot(w.astype(BF16), s_b)
        v_new_b = v_new.astype(BF16)
        o = _dot(q_dec.astype(BF16), s_b) + _dot(qk.astype(BF16), v_new_b)
        s_ref[h] = s * g_tot[:, h:h + 1] + _dot(k_dec.T.astype(BF16), v_new_b)
        gate = cur_ref[:, 3 * GDN_WIDTH + h * hd:3 * GDN_WIDTH + (h + 1) * hd]
        o_ref[:, h * hd:(h + 1) * hd] = (_rms(o) * onw_ref[...] * _silu(gate)).astype(o_ref.dtype)


def _gdn_mixer(proj, conv_w, a_log_row, dt_bias_row, out_norm_w):
    b, t, width = proj.shape
    c = MIX_CHUNK
    qkv = 3 * GDN_WIDTH
    halo_blocks = c // CONV_HALO
    return pl.pallas_call(
        _gdn_kernel,
        grid=(b, t // c),
        in_specs=[
            pl.BlockSpec((None, c, width), lambda i, j: (i, j, 0)),
            pl.BlockSpec((None, CONV_HALO, qkv), lambda i, j: (i, jnp.maximum(j * halo_blocks - 1, 0), 0)),
            pl.BlockSpec((CONV_K, qkv), lambda i, j: (0, 0)),
            pl.BlockSpec((1, LANE), lambda i, j: (0, 0)),
            pl.BlockSpec((1, LANE), lambda i, j: (0, 0)),
            pl.BlockSpec((1, GDN_HEAD_DIM), lambda i, j: (0, 0)),
        ],
        out_specs=pl.BlockSpec((None, c, GDN_WIDTH), lambda i, j: (i, j, 0)),
        out_shape=jax.ShapeDtypeStruct((b, t, GDN_WIDTH), BF16),
        scratch_shapes=[pltpu.VMEM((GDN_HEADS, GDN_HEAD_DIM, GDN_HEAD_DIM), F32)],
        compiler_params=_cparams(("parallel", "arbitrary")),
        name="gdn_mixer",
    )(proj, proj, conv_w, a_log_row, dt_bias_row, out_norm_w.reshape(1, GDN_HEAD_DIM))


def _ssd_kernel(cur_ref, prev_ref, cw_ref, cb_ref, alog_ref, dtb_ref, dskip_ref, onw_ref, expand_ref,
                o_ref, st_ref):
    t_idx = pl.program_id(1)

    @pl.when(t_idx == 0)
    def _():
        st_ref[...] = jnp.zeros_like(st_ref)

    c = cur_ref.shape[0]
    hp = SSM_HEAD_DIM
    gw = SSM_GROUP_WIDTH
    bc_lo = SSM_WIDTH
    z_lo = SSM_WIDTH + 2 * SSM_GROUPS * SSM_STATE
    small_lo = z_lo + SSM_WIDTH
    use_prev = jnp.where(t_idx > 0, 1.0, 0.0)
    row = lax.broadcasted_iota(jnp.int32, (c, c), 0)
    col = lax.broadcasted_iota(jnp.int32, (c, c), 1)
    causal = row >= col
    tril = jnp.where(causal, 1.0, 0.0)
    lane = lax.broadcasted_iota(jnp.int32, (c, LANE), 1)
    first_head = lane < hp

    small = cur_ref[:, small_lo:small_lo + LANE]
    dt = _softplus(small + dtb_ref[...])
    acs = _dot_f32(tril, dt * -jnp.exp(alog_ref[...]))
    acs_t = acs.T
    a_last = acs[c - 1:c, :]
    e_acs = jnp.exp(acs)
    e_rest = jnp.exp(a_last - acs)
    expand = expand_ref[...]
    dt_w = _dot_f32(dt, expand)
    e_acs_w = _dot_f32(e_acs, expand)
    e_rest_w = _dot_f32(e_rest, expand)

    def conv(lo, width):
        return _silu(_causal_conv(cur_ref, prev_ref, cw_ref, lo, width, use_prev) + cb_ref[:, lo:lo + width])

    for g in range(SSM_GROUPS):
        bg = conv(bc_lo + g * SSM_STATE, SSM_STATE)
        cg = conv(bc_lo + (SSM_GROUPS + g) * SSM_STATE, SSM_STATE)
        bg_b = bg.astype(BF16)
        cg_b = cg.astype(BF16)
        cb = _dot_nt(cg_b, bg_b)
        cols = slice(g * gw, (g + 1) * gw)
        xs = conv(g * gw, gw)
        xc = xs * dt_w[:, cols]
        state = st_ref[g]
        y = _dot(cg_b, state.astype(BF16)) * e_acs_w[:, cols]
        chunk_state = _dot(bg.T.astype(BF16), (xc * e_rest_w[:, cols]).astype(BF16))
        st_ref[g] = state * e_acs_w[c - 1:c, cols] + chunk_state
        pieces = []
        for pr in range(gw // LANE):
            xp = xc[:, pr * LANE:(pr + 1) * LANE]
            lhs = []
            for r in range(2):
                hidx = g * (SSM_HEADS // SSM_GROUPS) + 2 * pr + r
                seg = jnp.exp(jnp.where(causal, acs[:, hidx:hidx + 1] - acs_t[hidx:hidx + 1, :], -jnp.inf))
                lhs.append((cb * seg).astype(BF16))
            rhs = jnp.concatenate([jnp.where(first_head, xp, 0.0), jnp.where(first_head, 0.0, xp)], axis=0)
            pieces.append(_dot(jnp.concatenate(lhs, axis=1), rhs.astype(BF16)))
        y = y + jnp.concatenate(pieces, axis=1) + xs * dskip_ref[:, cols]
        y = y * _silu(cur_ref[:, z_lo + g * gw:z_lo + (g + 1) * gw])
        o_ref[:, cols] = (_rms(y) * onw_ref[:, cols]).astype(o_ref.dtype)


def _ssd_mixer(proj, conv_w, conv_b, a_log_row, dt_bias_row, d_skip_w, out_norm_w, expand):
    b, t, width = proj.shape
    c = MIX_CHUNK
    xbc = SSM_WIDTH + 2 * SSM_GROUPS * SSM_STATE
    halo_blocks = c // CONV_HALO
    const = lambda i, j: (0, 0)
    return pl.pallas_call(
        _ssd_kernel,
        grid=(b, t // c),
        in_specs=[
            pl.BlockSpec((None, c, width), lambda i, j: (i, j, 0)),
            pl.BlockSpec((None, CONV_HALO, xbc), lambda i, j: (i, jnp.maximum(j * halo_blocks - 1, 0), 0)),
            pl.BlockSpec((CONV_K, xbc), const),
            pl.BlockSpec((1, xbc), const),
            pl.BlockSpec((1, LANE), const),
            pl.BlockSpec((1, LANE), const),
            pl.BlockSpec((1, SSM_WIDTH), const),
            pl.BlockSpec((1, SSM_WIDTH), const),
            pl.BlockSpec((LANE, SSM_WIDTH), const),
        ],
        out_specs=pl.BlockSpec((None, c, SSM_WIDTH), lambda i, j: (i, j, 0)),
        out_shape=jax.ShapeDtypeStruct((b, t, SSM_WIDTH), BF16),
        scratch_shapes=[pltpu.VMEM((SSM_GROUPS, SSM_STATE, SSM_GROUP_WIDTH), F32)],
        compiler_params=_cparams(("parallel", "arbitrary")),
        name="ssd_mixer",
    )(proj, proj, conv_w, conv_b.reshape(1, xbc), a_log_row, dt_bias_row, d_skip_w, out_norm_w.reshape(1, SSM_WIDTH),
      expand)


def _out_proj_kernel(h_ref, og_ref, os_ref, wg_ref, ws_ref, o_ref):
    o_ref[...] = h_ref[...] + _dot(og_ref[...], wg_ref[...]) + _dot(os_ref[...], ws_ref[...])


def _out_proj(h, o_gdn, o_ssm, w_gdn, w_ssm, *, tm=512):
    n, d = h.shape
    const = lambda i: (0, 0)
    return pl.pallas_call(
        _out_proj_kernel,
        grid=(n // tm,),
        in_specs=[
            pl.BlockSpec((tm, d), lambda i: (i, 0)),
            pl.BlockSpec((tm, GDN_WIDTH), lambda i: (i, 0)),
            pl.BlockSpec((tm, SSM_WIDTH), lambda i: (i, 0)),
            pl.BlockSpec((GDN_WIDTH, d), const),
            pl.BlockSpec((SSM_WIDTH, d), const),
        ],
        out_specs=pl.BlockSpec((tm, d), lambda i: (i, 0)),
        out_shape=jax.ShapeDtypeStruct((n, d), F32),
        compiler_params=_cparams(("parallel",)),
        name="out_proj",
    )(h, o_gdn, o_ssm, w_gdn, w_ssm)


def _ple_kernel(h_ref, p_ref, nw_ref, wg_ref, wp_ref, pnw_ref, fnw_ref, o_ref):
    h = h_ref[...]
    gate = _sigmoid(_dot((_rms(h) * nw_ref[...]).astype(BF16), wg_ref[...]))
    emb = _rms(_dot(p_ref[...].astype(BF16), wp_ref[...])) * pnw_ref[...]
    o_ref[...] = _rms(h + gate * emb) * fnw_ref[...]


def _ple(h, p, norm_w, w_gate, w_proj, post_norm_w, final_norm_w, *, tm=256):
    n, d = h.shape
    pd = p.shape[1]
    const = lambda i: (0, 0)
    return pl.pallas_call(
        _ple_kernel,
        grid=(n // tm,),
        in_specs=[
            pl.BlockSpec((tm, d), lambda i: (i, 0)),
            pl.BlockSpec((tm, pd), lambda i: (i, 0)),
            pl.BlockSpec((1, d), const),
            pl.BlockSpec((d, d), const),
            pl.BlockSpec((pd, d), const),
            pl.BlockSpec((1, d), const),
            pl.BlockSpec((1, d), const),
        ],
        out_specs=pl.BlockSpec((tm, d), lambda i: (i, 0)),
        out_shape=jax.ShapeDtypeStruct((n, d), F32),
        compiler_params=_cparams(("parallel",)),
        name="ple_final",
    )(h, p, norm_w.reshape(1, d), w_gate, w_proj, post_norm_w.reshape(1, d), final_norm_w.reshape(1, d))


def _lane_row(v):
    return jnp.zeros((1, LANE), F32).at[0, :v.shape[0]].set(v.astype(F32))


def kernel(x, p, ffn1_norm, ffn1_w_gate, ffn1_w_up, ffn1_w_down, mix_norm, w_in, gdn_conv_w, gdn_a_log, gdn_dt_bias, gdn_out_norm, ssm_conv_w, ssm_conv_b, ssm_a_log, ssm_dt_bias, ssm_d, ssm_out_norm, w_out, ffn2_norm, ffn2_w_gate, ffn2_w_up, ffn2_w_down, ple_norm, ple_w_gate, ple_w_proj, ple_post_norm, final_norm):
    bsz, seq, d = x.shape
    n = bsz * seq
    depth = p.shape[0]
    xbc = SSM_WIDTH + 2 * SSM_GROUPS * SSM_STATE
    o_gate = 3 * GDN_WIDTH
    o_a = o_gate + GDN_WIDTH
    o_z = o_a + 2 * GDN_HEADS
    o_xbc = o_z + SSM_WIDTH
    o_dt = o_xbc + xbc
    expand = (jnp.arange(LANE)[:, None] == (jnp.arange(SSM_WIDTH)[None, :] // SSM_HEAD_DIM)).astype(F32)

    h = x.reshape(n, d)
    for i in range(depth):
        h = _ffn(h, ffn1_norm[i], ffn1_w_gate[i].astype(BF16), ffn1_w_up[i].astype(BF16),
                 ffn1_w_down[i].astype(BF16))
        wi = w_in[i]
        pad = jnp.zeros((d, LANE - 2 * GDN_HEADS), wi.dtype)
        w_gdn = jnp.concatenate([wi[:, :o_z], pad], axis=1).astype(BF16)
        pad = jnp.zeros((d, LANE - SSM_HEADS), wi.dtype)
        w_ssd = jnp.concatenate([wi[:, o_xbc:o_dt], wi[:, o_z:o_xbc], wi[:, o_dt:], pad], axis=1).astype(BF16)
        proj_gdn = _norm_matmul(h, mix_norm[i], w_gdn, tn=w_gdn.shape[1] // 3)
        proj_ssd = _norm_matmul(h, mix_norm[i], w_ssd, tn=w_ssd.shape[1] // 3)
        o_gdn = _gdn_mixer(proj_gdn.reshape(bsz, seq, -1), gdn_conv_w[i], _lane_row(gdn_a_log[i]),
                           _lane_row(gdn_dt_bias[i]), gdn_out_norm[i])
        o_ssm = _ssd_mixer(proj_ssd.reshape(bsz, seq, -1), ssm_conv_w[i], ssm_conv_b[i],
                           _lane_row(ssm_a_log[i]), _lane_row(ssm_dt_bias[i]),
                           jnp.repeat(ssm_d[i].astype(F32), SSM_HEAD_DIM).reshape(1, SSM_WIDTH),
                           ssm_out_norm[i], expand)
        wo = w_out[i].astype(BF16)
        h = _out_proj(h, o_gdn.reshape(n, GDN_WIDTH), o_ssm.reshape(n, SSM_WIDTH), wo[:GDN_WIDTH], wo[GDN_WIDTH:])
        h = _ffn(h, ffn2_norm[i], ffn2_w_gate[i].astype(BF16), ffn2_w_up[i].astype(BF16),
                 ffn2_w_down[i].astype(BF16))
        last = i == depth - 1
        assert last, "depth > 1 would need an un-normalised variant of the embedding kernel"
        h = _ple(h, p[i].reshape(n, -1), ple_norm[i], ple_w_gate[i].astype(BF16), ple_w_proj[i].astype(BF16),
                 ple_post_norm[i], final_norm)
    return h.reshape(bsz, seq, d)
```

```python
import functools

import jax
import jax.numpy as jnp
from jax import lax
from jax.experimental import pallas as pl
from jax.experimental.pallas import tpu as pltpu

F32 = jnp.float32
BF16 = jnp.bfloat16
EPS = 1e-6

LANE = 128
CONV_K = 4
CONV_HALO = 8
MIX_CHUNK = 128
GDN_HEADS = 8
GDN_HEAD_DIM = 128
GDN_WIDTH = GDN_HEADS * GDN_HEAD_DIM
SSM_HEADS = 16
SSM_HEAD_DIM = 64
SSM_WIDTH = SSM_HEADS * SSM_HEAD_DIM
SSM_GROUPS = 2
SSM_STATE = 128
SSM_GROUP_WIDTH = SSM_WIDTH // SSM_GROUPS
VMEM_LIMIT = 56 * 1024 * 1024


def _cparams(sem):
    return pltpu.CompilerParams(dimension_semantics=sem, vmem_limit_bytes=VMEM_LIMIT)


def _dot(a, b):
    return jnp.dot(a, b, preferred_element_type=F32)


def _dot_nt(a, b):
    return lax.dot_general(a, b, (((1,), (1,)), ((), ())), preferred_element_type=F32)


def _bdot(a, b):
    return _dot(a.astype(BF16), b.astype(BF16))


def _split3(a):
    a1 = a.astype(BF16)
    r1 = a - a1.astype(F32)
    a2 = r1.astype(BF16)
    a3 = (r1 - a2.astype(F32)).astype(BF16)
    return a1, a2, a3


def _dot_sel_lhs(sel, b):
    b1, b2, b3 = _split3(b)
    return _dot(sel, b1) + (_dot(sel, b2) + _dot(sel, b3))


def _dot_sel_rhs(a, sel):
    a1, a2, a3 = _split3(a)
    return _dot(a1, sel) + (_dot(a2, sel) + _dot(a3, sel))


def _sigmoid(x):
    return 1.0 / (1.0 + jnp.exp(-x))


def _silu(x):
    return x * _sigmoid(x)


def _softplus(x):
    return jnp.maximum(x, 0.0) + jnp.log1p(jnp.exp(-jnp.abs(x)))


def _rms(x):
    return x * lax.rsqrt(jnp.mean(x * x, axis=-1, keepdims=True) + EPS)


def _ffn_kernel(x_ref, nw_ref, wg_ref, wu_ref, wd_ref, nnw_ref, o_ref, on_ref, xn_ref, acc_ref):
    j = pl.program_id(1)

    @pl.when(j == 0)
    def _():
        xn_ref[...] = (_rms(x_ref[...]) * nw_ref[...]).astype(BF16)
        acc_ref[...] = jnp.zeros_like(acc_ref)

    xn = xn_ref[...]
    g = _dot(xn, wg_ref[...])
    u = _dot(xn, wu_ref[...])
    h = (_silu(g) * u).astype(BF16)
    acc_ref[...] += _dot(h, wd_ref[...])

    @pl.when(j == pl.num_programs(1) - 1)
    def _():
        out = x_ref[...] + 0.5 * acc_ref[...]
        o_ref[...] = out
        on_ref[...] = (_rms(out) * nnw_ref[...]).astype(BF16)


def _ffn(x, norm_w, wg, wu, wd, next_norm_w, *, tm=512, tf=512):
    n, d = x.shape
    f = wg.shape[1]
    return pl.pallas_call(
        _ffn_kernel,
        grid=(n // tm, f // tf),
        in_specs=[
            pl.BlockSpec((tm, d), lambda i, j: (i, 0)),
            pl.BlockSpec((1, d), lambda i, j: (0, 0)),
            pl.BlockSpec((d, tf), lambda i, j: (0, j)),
            pl.BlockSpec((d, tf), lambda i, j: (0, j)),
            pl.BlockSpec((tf, d), lambda i, j: (j, 0)),
            pl.BlockSpec((1, d), lambda i, j: (0, 0)),
        ],
        out_specs=[pl.BlockSpec((tm, d), lambda i, j: (i, 0)), pl.BlockSpec((tm, d), lambda i, j: (i, 0))],
        out_shape=[jax.ShapeDtypeStruct((n, d), F32), jax.ShapeDtypeStruct((n, d), BF16)],
        scratch_shapes=[pltpu.VMEM((tm, d), BF16), pltpu.VMEM((tm, d), F32)],
        compiler_params=_cparams(("parallel", "arbitrary")),
        name="ffn",
    )(x, norm_w.reshape(1, d), wg, wu, wd, next_norm_w.reshape(1, d))


def _matmul_kernel(x_ref, w_ref, o_ref):
    o_ref[...] = _dot(x_ref[...], w_ref[...])


def _matmul(xn, w, *, tm=1024, tn):
    n, d = xn.shape
    cols = w.shape[1]
    return pl.pallas_call(
        _matmul_kernel,
        grid=(cols // tn, n // tm),
        in_specs=[
            pl.BlockSpec((tm, d), lambda j, i: (i, 0)),
            pl.BlockSpec((d, tn), lambda j, i: (0, j)),
        ],
        out_specs=pl.BlockSpec((tm, tn), lambda j, i: (i, j)),
        out_shape=jax.ShapeDtypeStruct((n, cols), F32),
        compiler_params=_cparams(("parallel", "parallel")),
        name="in_proj",
    )(xn, w)


def _causal_conv(cur_ref, prev_ref, w_ref, lo, width, use_prev):
    c = cur_ref.shape[0]
    xx = jnp.concatenate([prev_ref[:, lo:lo + width] * use_prev, cur_ref[:, lo:lo + width]], axis=0)
    y = w_ref[CONV_K - 1:CONV_K, lo:lo + width] * xx[CONV_HALO:CONV_HALO + c]
    for j in range(CONV_K - 1):
        shifted = pltpu.roll(xx, CONV_K - 1 - j, axis=0)
        y = y + w_ref[j:j + 1, lo:lo + width] * shifted[CONV_HALO:CONV_HALO + c]
    return y


def _inv_unit_lower_many(mats, row, col):
    c = mats[0].shape[0]
    base = 16
    diag_blk = (row // base) == (col // base)
    eye = jnp.where(row == col, 1.0, 0.0)
    ps = [jnp.where(diag_blk, -a, 0.0) for a in mats]
    ts = [eye + p for p in ps]
    for _ in range(3):
        ps = [_bdot(p, p) for p in ps]
        ts = [t + _bdot(t, p) for t, p in zip(ts, ps)]
    b = 2 * base
    while b <= c:
        off = ((row // b) == (col // b)) & ((row // (b // 2)) != (col // (b // 2)))
        ets = [_bdot(jnp.where(off, a, 0.0), t) for a, t in zip(mats, ts)]
        ts = [t - _bdot(t, et) for t, et in zip(ts, ets)]
        b *= 2
    return ts


def _gdn_kernel(cur_ref, prev_ref, cw_ref, alog_ref, dtb_ref, onw_ref, o_ref, s_ref):
    t_idx = pl.program_id(1)

    @pl.when(t_idx == 0)
    def _():
        s_ref[...] = jnp.zeros_like(s_ref)

    c = cur_ref.shape[0]
    hd = GDN_HEAD_DIM
    use_prev = jnp.where(t_idx > 0, 1.0, 0.0)
    row = lax.broadcasted_iota(jnp.int32, (c, c), 0)
    col = lax.broadcasted_iota(jnp.int32, (c, c), 1)
    causal = row >= col
    strict = row > col
    tril = jnp.where(causal, 1.0, 0.0).astype(BF16)

    small = cur_ref[:, 4 * GDN_WIDTH:4 * GDN_WIDTH + LANE]
    g = -jnp.exp(alog_ref[...]) * _softplus(small + dtb_ref[...])
    beta = _sigmoid(small)
    gc = _dot_sel_lhs(tril, g)
    gc_t = gc.T
    g_last = gc[c - 1:c, :]
    e_gc = jnp.exp(gc)
    e_rest = jnp.exp(g_last - gc)
    g_tot = jnp.exp(g_last)

    heads = range(GDN_HEADS)

    def conv(lo):
        return _silu(_causal_conv(cur_ref, prev_ref, cw_ref, lo, hd, use_prev))

    q = [conv(h * hd) for h in heads]
    k = [conv(GDN_WIDTH + h * hd) for h in heads]
    v = [conv(2 * GDN_WIDTH + h * hd) for h in heads]
    q = [x * lax.rsqrt(jnp.sum(x * x, axis=-1, keepdims=True) + EPS) * (hd ** -0.5) for x in q]
    k = [x * lax.rsqrt(jnp.sum(x * x, axis=-1, keepdims=True) + EPS) for x in k]
    beta_c = [beta[:, GDN_HEADS + h:GDN_HEADS + h + 1] for h in heads]
    decay = [jnp.exp(jnp.where(causal, gc[:, h:h + 1] - gc_t[h:h + 1, :], -jnp.inf)) for h in heads]
    kb = [k[h] * beta_c[h] for h in heads]
    k_b = [x.astype(BF16) for x in k]
    a = [jnp.where(strict, _dot_nt(kb[h].astype(BF16), k_b[h]) * decay[h], 0.0) for h in heads]
    qk = [(_dot_nt(q[h].astype(BF16), k_b[h]) * decay[h]).astype(BF16) for h in heads]
    t_inv = _inv_unit_lower_many(a, row, col)
    uw = [_bdot(t_inv[h], jnp.concatenate([v[h] * beta_c[h], kb[h] * e_gc[:, h:h + 1]], axis=1)) for h in heads]
    s = [s_ref[h] for h in heads]
    wq = [jnp.concatenate([uw[h][:, hd:], q[h] * e_gc[:, h:h + 1]], axis=0) for h in heads]
    wq_s = [_bdot(wq[h], s[h]) for h in heads]
    v_new = [(uw[h][:, :hd] - wq_s[h][:c]).astype(BF16) for h in heads]
    o = [wq_s[h][c:] + _dot(qk[h], v_new[h]) for h in heads]
    k_dec_t = [(k[h] * e_rest[:, h:h + 1]).T.astype(BF16) for h in heads]
    for h in heads:
        s_ref[h] = s[h] * g_tot[:, h:h + 1] + _dot(k_dec_t[h], v_new[h])
    for h in heads:
        gate = cur_ref[:, 3 * GDN_WIDTH + h * hd:3 * GDN_WIDTH + (h + 1) * hd]
        o_ref[:, h * hd:(h + 1) * hd] = (_rms(o[h]) * onw_ref[...] * _silu(gate)).astype(o_ref.dtype)


def _gdn_mixer(proj, conv_w, a_log_row, dt_bias_row, out_norm_w):
    b, t, width = proj.shape
    c = MIX_CHUNK
    qkv = 3 * GDN_WIDTH
    halo_blocks = c // CONV_HALO
    return pl.pallas_call(
        _gdn_kernel,
        grid=(b, t // c),
        in_specs=[
            pl.BlockSpec((None, c, width), lambda i, j: (i, j, 0)),
            pl.BlockSpec((None, CONV_HALO, qkv), lambda i, j: (i, jnp.maximum(j * halo_blocks - 1, 0), 0)),
            pl.BlockSpec((CONV_K, qkv), lambda i, j: (0, 0)),
            pl.BlockSpec((1, LANE), lambda i, j: (0, 0)),
            pl.BlockSpec((1, LANE), lambda i, j: (0, 0)),
            pl.BlockSpec((1, GDN_HEAD_DIM), lambda i, j: (0, 0)),
        ],
        out_specs=pl.BlockSpec((None, c, GDN_WIDTH), lambda i, j: (i, j, 0)),
        out_shape=jax.ShapeDtypeStruct((b, t, GDN_WIDTH), BF16),
        scratch_shapes=[pltpu.VMEM((GDN_HEADS, GDN_HEAD_DIM, GDN_HEAD_DIM), F32)],
        compiler_params=_cparams(("parallel", "arbitrary")),
        name="gdn_mixer",
    )(proj, proj, conv_w, a_log_row, dt_bias_row, out_norm_w.reshape(1, GDN_HEAD_DIM))


def _ssd_kernel(cur_ref, prev_ref, cw_ref, cb_ref, alog_ref, dtb_ref, dskip_ref, onw_ref, expand_ref,
                o_ref, st_ref):
    t_idx = pl.program_id(1)

    @pl.when(t_idx == 0)
    def _():
        st_ref[...] = jnp.zeros_like(st_ref)

    c = cur_ref.shape[0]
    hp = SSM_HEAD_DIM
    gw = SSM_GROUP_WIDTH
    bc_lo = SSM_WIDTH
    z_lo = SSM_WIDTH + 2 * SSM_GROUPS * SSM_STATE
    small_lo = z_lo + SSM_WIDTH
    use_prev = jnp.where(t_idx > 0, 1.0, 0.0)
    row = lax.broadcasted_iota(jnp.int32, (c, c), 0)
    col = lax.broadcasted_iota(jnp.int32, (c, c), 1)
    causal = row >= col
    tril = jnp.where(causal, 1.0, 0.0).astype(BF16)
    lane = lax.broadcasted_iota(jnp.int32, (c, LANE), 1)
    first_head = lane < hp

    small = cur_ref[:, small_lo:small_lo + LANE]
    dt = _softplus(small + dtb_ref[...])
    acs = _dot_sel_lhs(tril, dt * -jnp.exp(alog_ref[...]))
    acs_t = acs.T
    expand = expand_ref[...]
    dt_w = _dot_sel_rhs(dt, expand)
    acs_w = _dot_sel_rhs(acs, expand)
    e_acs_w = jnp.exp(acs_w)
    e_rest_w = jnp.exp(acs_w[c - 1:c, :] - acs_w)

    def conv(lo, width):
        return _silu(_causal_conv(cur_ref, prev_ref, cw_ref, lo, width, use_prev) + cb_ref[:, lo:lo + width])

    for g in range(SSM_GROUPS):
        bg = conv(bc_lo + g * SSM_STATE, SSM_STATE)
        cg = conv(bc_lo + (SSM_GROUPS + g) * SSM_STATE, SSM_STATE)
        bg_b = bg.astype(BF16)
        cg_b = cg.astype(BF16)
        cb = _dot_nt(cg_b, bg_b)
        cols = slice(g * gw, (g + 1) * gw)
        xs = conv(g * gw, gw)
        xc = xs * dt_w[:, cols]
        state = st_ref[g]
        y = _dot(cg_b, state.astype(BF16)) * e_acs_w[:, cols]
        chunk_state = _dot(bg.T.astype(BF16), (xc * e_rest_w[:, cols]).astype(BF16))
        st_ref[g] = state * e_acs_w[c - 1:c, cols] + chunk_state
        pieces = []
        for pr in range(gw // LANE):
            xp = xc[:, pr * LANE:(pr + 1) * LANE]
            lhs = []
            for r in range(2):
                hidx = g * (SSM_HEADS // SSM_GROUPS) + 2 * pr + r
                seg = jnp.exp(jnp.where(causal, acs[:, hidx:hidx + 1] - acs_t[hidx:hidx + 1, :], -jnp.inf))
                lhs.append((cb * seg).astype(BF16))
            rhs = jnp.concatenate([jnp.where(first_head, xp, 0.0), jnp.where(first_head, 0.0, xp)], axis=0)
            pieces.append(_dot(jnp.concatenate(lhs, axis=1), rhs.astype(BF16)))
        y = y + jnp.concatenate(pieces, axis=1) + xs * dskip_ref[:, cols]
        y = y * _silu(cur_ref[:, z_lo + g * gw:z_lo + (g + 1) * gw])
        o_ref[:, cols] = (_rms(y) * onw_ref[:, cols]).astype(o_ref.dtype)


def _ssd_mixer(proj, conv_w, conv_b, a_log_row, dt_bias_row, d_skip_w, out_norm_w, expand):
    b, t, width = proj.shape
    c = MIX_CHUNK
    xbc = SSM_WIDTH + 2 * SSM_GROUPS * SSM_STATE
    halo_blocks = c // CONV_HALO
    const = lambda i, j: (0, 0)
    return pl.pallas_call(
        _ssd_kernel,
        grid=(b, t // c),
        in_specs=[
            pl.BlockSpec((None, c, width), lambda i, j: (i, j, 0)),
            pl.BlockSpec((None, CONV_HALO, xbc), lambda i, j: (i, jnp.maximum(j * halo_blocks - 1, 0), 0)),
            pl.BlockSpec((CONV_K, xbc), const),
            pl.BlockSpec((1, xbc), const),
            pl.BlockSpec((1, LANE), const),
            pl.BlockSpec((1, LANE), const),
            pl.BlockSpec((1, SSM_WIDTH), const),
            pl.BlockSpec((1, SSM_WIDTH), const),
            pl.BlockSpec((LANE, SSM_WIDTH), const),
        ],
        out_specs=pl.BlockSpec((None, c, SSM_WIDTH), lambda i, j: (i, j, 0)),
        out_shape=jax.ShapeDtypeStruct((b, t, SSM_WIDTH), BF16),
        scratch_shapes=[pltpu.VMEM((SSM_GROUPS, SSM_STATE, SSM_GROUP_WIDTH), F32)],
        compiler_params=_cparams(("parallel", "arbitrary")),
        name="ssd_mixer",
    )(proj, proj, conv_w, conv_b.reshape(1, xbc), a_log_row, dt_bias_row, d_skip_w, out_norm_w.reshape(1, SSM_WIDTH),
      expand)


def _out_proj_kernel(h_ref, og_ref, os_ref, wg_ref, ws_ref, o_ref):
    o_ref[...] = h_ref[...] + _dot(og_ref[...], wg_ref[...]) + _dot(os_ref[...], ws_ref[...])


def _out_proj(h, o_gdn, o_ssm, w_gdn, w_ssm, *, tm=512):
    n, d = h.shape
    const = lambda i: (0, 0)
    return pl.pallas_call(
        _out_proj_kernel,
        grid=(n // tm,),
        in_specs=[
            pl.BlockSpec((tm, d), lambda i: (i, 0)),
            pl.BlockSpec((tm, GDN_WIDTH), lambda i: (i, 0)),
            pl.BlockSpec((tm, SSM_WIDTH), lambda i: (i, 0)),
            pl.BlockSpec((GDN_WIDTH, d), const),
            pl.BlockSpec((SSM_WIDTH, d), const),
        ],
        out_specs=pl.BlockSpec((tm, d), lambda i: (i, 0)),
        out_shape=jax.ShapeDtypeStruct((n, d), F32),
        compiler_params=_cparams(("parallel",)),
        name="out_proj",
    )(h, o_gdn, o_ssm, w_gdn, w_ssm)


def _ple_kernel(h_ref, hn_ref, p_ref, wg_ref, wp_ref, pnw_ref, fnw_ref, o_ref):
    gate = _sigmoid(_dot(hn_ref[...], wg_ref[...]))
    emb = _rms(_dot(p_ref[...].astype(BF16), wp_ref[...])) * pnw_ref[...]
    o_ref[...] = _rms(h_ref[...] + gate * emb) * fnw_ref[...]


def _ple(h, hn, p, w_gate, w_proj, post_norm_w, final_norm_w, *, tm=256):
    n, d = h.shape
    pd = p.shape[1]
    const = lambda i: (0, 0)
    return pl.pallas_call(
        _ple_kernel,
        grid=(n // tm,),
        in_specs=[
            pl.BlockSpec((tm, d), lambda i: (i, 0)),
            pl.BlockSpec((tm, d), lambda i: (i, 0)),
            pl.BlockSpec((tm, pd), lambda i: (i, 0)),
            pl.BlockSpec((d, d), const),
            pl.BlockSpec((pd, d), const),
            pl.BlockSpec((1, d), const),
            pl.BlockSpec((1, d), const),
        ],
        out_specs=pl.BlockSpec((tm, d), lambda i: (i, 0)),
        out_shape=jax.ShapeDtypeStruct((n, d), F32),
        compiler_params=_cparams(("parallel",)),
        name="ple_final",
    )(h, hn, p, w_gate, w_proj, post_norm_w.reshape(1, d), final_norm_w.reshape(1, d))


def _lane_row(v):
    return jnp.zeros((1, LANE), F32).at[0, :v.shape[0]].set(v.astype(F32))


def kernel(x, p, ffn1_norm, ffn1_w_gate, ffn1_w_up, ffn1_w_down, mix_norm, w_in, gdn_conv_w, gdn_a_log, gdn_dt_bias, gdn_out_norm, ssm_conv_w, ssm_conv_b, ssm_a_log, ssm_dt_bias, ssm_d, ssm_out_norm, w_out, ffn2_norm, ffn2_w_gate, ffn2_w_up, ffn2_w_down, ple_norm, ple_w_gate, ple_w_proj, ple_post_norm, final_norm):
    bsz, seq, d = x.shape
    n = bsz * seq
    depth = p.shape[0]
    xbc = SSM_WIDTH + 2 * SSM_GROUPS * SSM_STATE
    o_gate = 3 * GDN_WIDTH
    o_a = o_gate + GDN_WIDTH
    o_z = o_a + 2 * GDN_HEADS
    o_xbc = o_z + SSM_WIDTH
    o_dt = o_xbc + xbc
    expand = (jnp.arange(LANE)[:, None] == (jnp.arange(SSM_WIDTH)[None, :] // SSM_HEAD_DIM)).astype(BF16)

    h = x.reshape(n, d)
    for i in range(depth):
        h, hn = _ffn(h, ffn1_norm[i], ffn1_w_gate[i].astype(BF16), ffn1_w_up[i].astype(BF16),
                     ffn1_w_down[i].astype(BF16), mix_norm[i])
        wi = w_in[i]
        pad = jnp.zeros((d, LANE - 2 * GDN_HEADS), wi.dtype)
        w_gdn = jnp.concatenate([wi[:, :o_z], pad], axis=1).astype(BF16)
        pad = jnp.zeros((d, LANE - SSM_HEADS), wi.dtype)
        w_ssd = jnp.concatenate([wi[:, o_xbc:o_dt], wi[:, o_z:o_xbc], wi[:, o_dt:], pad], axis=1).astype(BF16)
        proj_gdn = _matmul(hn, w_gdn, tn=w_gdn.shape[1] // 3)
        proj_ssd = _matmul(hn, w_ssd, tn=w_ssd.shape[1] // 3)
        o_gdn = _gdn_mixer(proj_gdn.reshape(bsz, seq, -1), gdn_conv_w[i], _lane_row(gdn_a_log[i]),
                           _lane_row(gdn_dt_bias[i]), gdn_out_norm[i])
        o_ssm = _ssd_mixer(proj_ssd.reshape(bsz, seq, -1), ssm_conv_w[i], ssm_conv_b[i],
                           _lane_row(ssm_a_log[i]), _lane_row(ssm_dt_bias[i]),
                           jnp.repeat(ssm_d[i].astype(F32), SSM_HEAD_DIM).reshape(1, SSM_WIDTH),
                           ssm_out_norm[i], expand)
        wo = w_out[i].astype(BF16)
        h = _out_proj(h, o_gdn.reshape(n, GDN_WIDTH), o_ssm.reshape(n, SSM_WIDTH), wo[:GDN_WIDTH], wo[GDN_WIDTH:])
        h, hn = _ffn(h, ffn2_norm[i], ffn2_w_gate[i].astype(BF16), ffn2_w_up[i].astype(BF16),
                     ffn2_w_down[i].astype(BF16), ple_norm[i])
        assert i == depth - 1, "depth > 1 would need an un-normalised variant of the embedding kernel"
        h = _ple(h, hn, p[i].reshape(n, -1), ple_w_gate[i].astype(BF16), ple_w_proj[i].astype(BF16),
                 ple_post_norm[i], final_norm)
    return h.reshape(bsz, seq, d)
```

```python
import functools

import jax
import jax.numpy as jnp
from jax import lax
from jax.experimental import pallas as pl
from jax.experimental.pallas import tpu as pltpu

F32 = jnp.float32
BF16 = jnp.bfloat16
EPS = 1e-6

LANE = 128
CONV_K = 4
CONV_HALO = 8
MIX_CHUNK = 128
GDN_HEADS = 8
GDN_HEAD_DIM = 128
GDN_WIDTH = GDN_HEADS * GDN_HEAD_DIM
GDN_HEAD_GROUP = 8
SSM_HEADS = 16
SSM_HEAD_DIM = 64
SSM_WIDTH = SSM_HEADS * SSM_HEAD_DIM
SSM_GROUPS = 2
SSM_STATE = 128
SSM_GROUP_WIDTH = SSM_WIDTH // SSM_GROUPS
VMEM_LIMIT = 56 * 1024 * 1024


def _cparams(sem):
    return pltpu.CompilerParams(dimension_semantics=sem, vmem_limit_bytes=VMEM_LIMIT)


def _dot(a, b):
    return jnp.dot(a, b, preferred_element_type=F32)


def _dot_nt(a, b):
    return lax.dot_general(a, b, (((1,), (1,)), ((), ())), preferred_element_type=F32)


def _bdot(a, b):
    return _dot(a.astype(BF16), b.astype(BF16))


def _split3(a):
    a1 = a.astype(BF16)
    r1 = a - a1.astype(F32)
    a2 = r1.astype(BF16)
    a3 = (r1 - a2.astype(F32)).astype(BF16)
    return a1, a2, a3


def _dot_sel_lhs(sel, b):
    b1, b2, b3 = _split3(b)
    return _dot(sel, b1) + (_dot(sel, b2) + _dot(sel, b3))


def _dot_sel_rhs(a, sel):
    a1, a2, a3 = _split3(a)
    return _dot(a1, sel) + (_dot(a2, sel) + _dot(a3, sel))


def _sigmoid(x):
    return 1.0 / (1.0 + jnp.exp(-x))


def _silu(x):
    return x * _sigmoid(x)


def _softplus(x):
    return jnp.maximum(x, 0.0) + jnp.log1p(jnp.exp(-jnp.abs(x)))


def _rms(x):
    return x * lax.rsqrt(jnp.mean(x * x, axis=-1, keepdims=True) + EPS)


def _ffn_kernel(x_ref, nw_ref, wg_ref, wu_ref, wd_ref, nnw_ref, *rest, cast_once, slabs):
    n_cast = len(cast_once)
    cast_in = rest[:n_cast]
    o_ref, on_ref = rest[n_cast:n_cast + 2]
    cast_out = rest[n_cast + 2:2 * n_cast + 2]
    (xn_ref,) = rest[2 * n_cast + 2:]
    j = pl.program_id(1)
    last = pl.num_programs(1) - 1
    tm = x_ref.shape[0]
    slab_rows = [pl.ds(r * (tm // slabs), tm // slabs) for r in range(slabs)]

    for src, dst, once in zip(cast_in, cast_out, cast_once):
        if once:
            @pl.when(j == 0)
            def _(src=src, dst=dst):
                dst[...] = src[...].astype(BF16)
        else:
            dst[...] = src[...].astype(BF16)

    def step(rows, first, final):
        if first:
            base = x_ref[rows, :]
            xn = (_rms(base) * nw_ref[...]).astype(BF16)
            xn_ref[rows, :] = xn
        else:
            base = o_ref[rows, :]
            xn = xn_ref[rows, :]
        g = _dot(xn, wg_ref[...])
        u = _dot(xn, wu_ref[...])
        h = (0.5 * _silu(g) * u).astype(BF16)
        out = base + _dot(h, wd_ref[...])
        o_ref[rows, :] = out
        if final:
            on_ref[rows, :] = (_rms(out) * nnw_ref[...]).astype(BF16)

    @pl.when(j == 0)
    def _():
        for rows in slab_rows:
            step(rows, True, False)

    @pl.when((j > 0) & (j < last))
    def _():
        step(slice(None), False, False)

    @pl.when(j == last)
    def _():
        for rows in slab_rows:
            step(rows, False, True)


def _ffn(x, norm_w, wg, wu, wd, next_norm_w, cast_jobs=(), *, tm=512, tf=512, slabs=2):
    n, d = x.shape
    f = wg.shape[1]
    grid = (n // tm, f // tf)
    assert grid[1] >= 2
    cast_specs = [pl.BlockSpec(blk, imap) for _, blk, imap, _ in cast_jobs]
    row_spec = pl.BlockSpec((tm, d), lambda i, j: (i, 0))
    vec_spec = pl.BlockSpec((1, d), lambda i, j: (0, 0))
    return pl.pallas_call(
        functools.partial(_ffn_kernel, cast_once=tuple(once for _, _, _, once in cast_jobs), slabs=slabs),
        grid=grid,
        in_specs=[
            row_spec,
            vec_spec,
            pl.BlockSpec((d, tf), lambda i, j: (0, j)),
            pl.BlockSpec((d, tf), lambda i, j: (0, j)),
            pl.BlockSpec((tf, d), lambda i, j: (j, 0)),
            vec_spec,
        ] + cast_specs,
        out_specs=[row_spec, row_spec] + cast_specs,
        out_shape=[jax.ShapeDtypeStruct((n, d), F32), jax.ShapeDtypeStruct((n, d), BF16)]
        + [jax.ShapeDtypeStruct(job[0].shape, BF16) for job in cast_jobs],
        scratch_shapes=[pltpu.VMEM((tm, d), BF16)],
        compiler_params=_cparams(("parallel", "arbitrary")),
        name="ffn",
    )(x, norm_w.reshape(1, d), wg, wu, wd, next_norm_w.reshape(1, d), *[job[0] for job in cast_jobs])


def _cast_jobs(grid, mats):
    gi, gj = grid
    jobs = []
    for a in mats:
        r, c = a.shape
        if c % LANE != 0:
            assert r % (gi * 8) == 0
            jobs.append((a, (r // gi, c), lambda i, j: (i, 0), True))
        elif r % (gi * 8) == 0 and c % (gj * LANE) == 0:
            jobs.append((a, (r // gi, c // gj), lambda i, j: (i, j), False))
        elif r % (gj * 8) == 0 and c % (gi * LANE) == 0:
            jobs.append((a, (r // gj, c // gi), lambda i, j: (j, i), False))
        else:
            nb = max(k for k in range(1, gj + 1) if c % (k * LANE) == 0)
            assert r % (gi * 8) == 0
            jobs.append((a, (r // gi, c // nb), lambda i, j, nb=nb: (i, jnp.minimum(j, nb - 1)), False))
    return jobs


def _matmul_nt_kernel(x_ref, w_ref, o_ref):
    o_ref[...] = _dot_nt(x_ref[...], w_ref[...])


def _matmul_nt(xn, wt, *, cols=None, tm=1024, tn):
    n, d = xn.shape
    cols = wt.shape[0] if cols is None else cols
    assert cols % tn == 0 and cols <= wt.shape[0]
    return pl.pallas_call(
        _matmul_nt_kernel,
        grid=(cols // tn, n // tm),
        in_specs=[
            pl.BlockSpec((tm, d), lambda j, i: (i, 0)),
            pl.BlockSpec((tn, d), lambda j, i: (j, 0)),
        ],
        out_specs=pl.BlockSpec((tm, tn), lambda j, i: (i, j)),
        out_shape=jax.ShapeDtypeStruct((n, cols), F32),
        compiler_params=_cparams(("parallel", "parallel")),
        name="in_proj",
    )(xn, wt)


def _causal_conv(halo, cur, w_ref, w_lo):
    c, width = cur.shape
    xx = jnp.concatenate([halo, cur], axis=0)
    y = w_ref[CONV_K - 1:CONV_K, w_lo:w_lo + width] * cur
    for j in range(CONV_K - 1):
        shifted = pltpu.roll(xx, CONV_K - 1 - j, axis=0)
        y = y + w_ref[j:j + 1, w_lo:w_lo + width] * shifted[CONV_HALO:CONV_HALO + c]
    return y


def _halo_rows(cur_ref, prev_ref, r0, cols, use_prev):
    if r0 == 0:
        return prev_ref[:, cols] * use_prev
    return cur_ref[r0 - CONV_HALO:r0, cols]


def _inv_unit_lower_many(mats, row, col):
    c = mats[0].shape[0]
    base = 16
    diag_blk = (row // base) == (col // base)
    eye = jnp.where(row == col, 1.0, 0.0)
    ps = [jnp.where(diag_blk, -a, 0.0) for a in mats]
    ts = [eye + p for p in ps]
    for _ in range(3):
        ps = [_bdot(p, p) for p in ps]
        ts = [t + _bdot(t, p) for t, p in zip(ts, ps)]
    b = 2 * base
    while b <= c:
        off = ((row // b) == (col // b)) & ((row // (b // 2)) != (col // (b // 2)))
        ets = [_bdot(jnp.where(off, a, 0.0), t) for a, t in zip(mats, ts)]
        ts = [t - _bdot(t, et) for t, et in zip(ts, ets)]
        b *= 2
    return ts


def _gdn_kernel(cur_ref, prev_ref, cw_ref, alog_ref, dtb_ref, onw_ref, o_ref, s_ref):
    t_idx = pl.program_id(1)

    @pl.when(t_idx == 0)
    def _():
        s_ref[...] = jnp.zeros_like(s_ref)

    c = MIX_CHUNK
    hd = GDN_HEAD_DIM
    use_prev = jnp.where(t_idx > 0, 1.0, 0.0)
    row = lax.broadcasted_iota(jnp.int32, (c, c), 0)
    col = lax.broadcasted_iota(jnp.int32, (c, c), 1)
    for r0 in range(0, cur_ref.shape[0], c):
        _gdn_chunk(r0, cur_ref, prev_ref, cw_ref, alog_ref, dtb_ref, onw_ref, o_ref, s_ref, use_prev, row, col)


def _gdn_chunk(r0, cur_ref, prev_ref, cw_ref, alog_ref, dtb_ref, onw_ref, o_ref, s_ref, use_prev, row, col):
    c = MIX_CHUNK
    hd = GDN_HEAD_DIM
    rows = slice(r0, r0 + c)
    causal = row >= col
    strict = row > col
    tril = jnp.where(causal, 1.0, 0.0).astype(BF16)

    small = cur_ref[rows, 4 * GDN_WIDTH:4 * GDN_WIDTH + LANE]
    g = -jnp.exp(alog_ref[...]) * _softplus(small + dtb_ref[...])
    beta = _sigmoid(small)
    gc = _dot_sel_lhs(tril, g)
    gc_t = gc.T
    g_last = gc[c - 1:c, :]
    e_gc = jnp.exp(gc)
    e_rest = jnp.exp(g_last - gc)
    g_tot = jnp.exp(g_last)

    def conv(lo):
        cols = slice(lo, lo + hd)
        halo = _halo_rows(cur_ref, prev_ref, r0, cols, use_prev)
        return _silu(_causal_conv(halo, cur_ref[rows, cols], cw_ref, lo))

    def l2norm(x):
        return x * lax.rsqrt(jnp.sum(x * x, axis=-1, keepdims=True) + EPS)

    for g0 in range(0, GDN_HEADS, GDN_HEAD_GROUP):
        heads = range(g0, g0 + GDN_HEAD_GROUP)
        q = {h: l2norm(conv(h * hd)) * (hd ** -0.5) for h in heads}
        k = {h: l2norm(conv(GDN_WIDTH + h * hd)) for h in heads}
        v = {h: conv(2 * GDN_WIDTH + h * hd) for h in heads}
        beta_c = {h: beta[:, GDN_HEADS + h:GDN_HEADS + h + 1] for h in heads}
        decay = {h: jnp.exp(jnp.where(causal, gc[:, h:h + 1] - gc_t[h:h + 1, :], -jnp.inf)) for h in heads}
        kb = {h: k[h] * beta_c[h] for h in heads}
        k_b = {h: k[h].astype(BF16) for h in heads}
        a = {h: jnp.where(strict, _dot_nt(kb[h].astype(BF16), k_b[h]) * decay[h], 0.0) for h in heads}
        qk = {h: (_dot_nt(q[h].astype(BF16), k_b[h]) * decay[h]).astype(BF16) for h in heads}
        t_inv = dict(zip(heads, _inv_unit_lower_many([a[h] for h in heads], row, col)))
        uw = {h: _bdot(t_inv[h], jnp.concatenate([v[h] * beta_c[h], kb[h] * e_gc[:, h:h + 1]], axis=1))
              for h in heads}
        s = {h: s_ref[h] for h in heads}
        wq = {h: jnp.concatenate([uw[h][:, hd:], q[h] * e_gc[:, h:h + 1]], axis=0) for h in heads}
        wq_s = {h: _bdot(wq[h], s[h]) for h in heads}
        v_new = {h: (uw[h][:, :hd] - wq_s[h][:c]).astype(BF16) for h in heads}
        o = {h: wq_s[h][c:] + _dot(qk[h], v_new[h]) for h in heads}
        k_dec_t = {h: (k[h] * e_rest[:, h:h + 1]).T.astype(BF16) for h in heads}
        for h in heads:
            s_ref[h] = s[h] * g_tot[:, h:h + 1] + _dot(k_dec_t[h], v_new[h])
        for h in heads:
            gate = cur_ref[rows, 3 * GDN_WIDTH + h * hd:3 * GDN_WIDTH + (h + 1) * hd]
            o_ref[rows, h * hd:(h + 1) * hd] = (_rms(o[h]) * onw_ref[...] * _silu(gate)).astype(o_ref.dtype)


def _gdn_mixer(proj, conv_w, a_log_row, dt_bias_row, out_norm_w, *, chunks_per_block=2):
    b, t, width = proj.shape
    c = chunks_per_block * MIX_CHUNK
    qkv = 3 * GDN_WIDTH
    halo_blocks = c // CONV_HALO
    return pl.pallas_call(
        _gdn_kernel,
        grid=(b, t // c),
        in_specs=[
            pl.BlockSpec((None, c, width), lambda i, j: (i, j, 0)),
            pl.BlockSpec((None, CONV_HALO, qkv), lambda i, j: (i, jnp.maximum(j * halo_blocks - 1, 0), 0)),
            pl.BlockSpec((CONV_K, qkv), lambda i, j: (0, 0)),
            pl.BlockSpec((1, LANE), lambda i, j: (0, 0)),
            pl.BlockSpec((1, LANE), lambda i, j: (0, 0)),
            pl.BlockSpec((1, GDN_HEAD_DIM), lambda i, j: (0, 0)),
        ],
        out_specs=pl.BlockSpec((None, c, GDN_WIDTH), lambda i, j: (i, j, 0)),
        out_shape=jax.ShapeDtypeStruct((b, t, GDN_WIDTH), BF16),
        scratch_shapes=[pltpu.VMEM((GDN_HEADS, GDN_HEAD_DIM, GDN_HEAD_DIM), F32)],
        compiler_params=_cparams(("parallel", "arbitrary")),
        name="gdn_mixer",
    )(proj, proj, conv_w, a_log_row, dt_bias_row, out_norm_w.reshape(1, GDN_HEAD_DIM))


def _ssd_kernel(cur_ref, prev_ref, cw_ref, cb_ref, alog_ref, dtb_ref, dskip_ref, onw_ref, expand_ref,
                o_ref, st_ref):
    t_idx = pl.program_id(1)

    @pl.when(t_idx == 0)
    def _():
        st_ref[...] = jnp.zeros_like(st_ref)

    c = MIX_CHUNK
    hp = SSM_HEAD_DIM
    gw = SSM_GROUP_WIDTH
    z_lo = 0
    x_lo = SSM_WIDTH
    small_lo = x_lo + SSM_WIDTH + 2 * SSM_GROUPS * SSM_STATE
    use_prev = jnp.where(t_idx > 0, 1.0, 0.0)
    row = lax.broadcasted_iota(jnp.int32, (c, c), 0)
    col = lax.broadcasted_iota(jnp.int32, (c, c), 1)
    causal = row >= col
    tril = jnp.where(causal, 1.0, 0.0).astype(BF16)
    lane = lax.broadcasted_iota(jnp.int32, (c, LANE), 1)
    first_head = lane < hp
    expand = expand_ref[...]

    for r0 in range(0, cur_ref.shape[0], c):
        rows = slice(r0, r0 + c)
        small = cur_ref[rows, small_lo:small_lo + LANE]
        dt = _softplus(small + dtb_ref[...])
        acs = _dot_sel_lhs(tril, dt * -jnp.exp(alog_ref[...]))
        acs_t = acs.T
        dt_w = _dot_sel_rhs(dt, expand)
        acs_w = _dot_sel_rhs(acs, expand)
        e_acs_w = jnp.exp(acs_w)
        e_rest_w = jnp.exp(acs_w[c - 1:c, :] - acs_w)

        def conv(lo, width, r0=r0, rows=rows):
            cols = slice(x_lo + lo, x_lo + lo + width)
            halo = _halo_rows(cur_ref, prev_ref, r0, cols, use_prev)
            return _silu(_causal_conv(halo, cur_ref[rows, cols], cw_ref, lo) + cb_ref[:, lo:lo + width])

        for g in range(SSM_GROUPS):
            bg = conv(SSM_WIDTH + g * SSM_STATE, SSM_STATE)
            cg = conv(SSM_WIDTH + (SSM_GROUPS + g) * SSM_STATE, SSM_STATE)
            bg_b = bg.astype(BF16)
            cg_b = cg.astype(BF16)
            cb = _dot_nt(cg_b, bg_b)
            cols = slice(g * gw, (g + 1) * gw)
            xs = conv(g * gw, gw)
            xc = xs * dt_w[:, cols]
            state = st_ref[g]
            y = _dot(cg_b, state.astype(BF16)) * e_acs_w[:, cols]
            chunk_state = _dot(bg.T.astype(BF16), (xc * e_rest_w[:, cols]).astype(BF16))
            st_ref[g] = state * e_acs_w[c - 1:c, cols] + chunk_state
            pieces = []
            for pr in range(gw // LANE):
                xp = xc[:, pr * LANE:(pr + 1) * LANE]
                lhs = []
                for r in range(2):
                    hidx = g * (SSM_HEADS // SSM_GROUPS) + 2 * pr + r
                    seg = jnp.exp(jnp.where(causal, acs[:, hidx:hidx + 1] - acs_t[hidx:hidx + 1, :], -jnp.inf))
                    lhs.append((cb * seg).astype(BF16))
                rhs = jnp.concatenate([jnp.where(first_head, xp, 0.0), jnp.where(first_head, 0.0, xp)], axis=0)
                pieces.append(_dot(jnp.concatenate(lhs, axis=1), rhs.astype(BF16)))
            y = y + jnp.concatenate(pieces, axis=1) + xs * dskip_ref[:, cols]
            y = y * _silu(cur_ref[rows, z_lo + g * gw:z_lo + (g + 1) * gw])
            o_ref[rows, cols] = (_rms(y) * onw_ref[:, cols]).astype(o_ref.dtype)


def _ssd_mixer(proj, conv_w, conv_b, a_log_row, dt_bias_row, d_skip_w, out_norm_w, expand, *, chunks_per_block=2):
    b, t, width = proj.shape
    c = chunks_per_block * MIX_CHUNK
    xbc = SSM_WIDTH + 2 * SSM_GROUPS * SSM_STATE
    halo_blocks = c // CONV_HALO
    const = lambda i, j: (0, 0)
    return pl.pallas_call(
        _ssd_kernel,
        grid=(b, t // c),
        in_specs=[
            pl.BlockSpec((None, c, width), lambda i, j: (i, j, 0)),
            pl.BlockSpec((None, CONV_HALO, width), lambda i, j: (i, jnp.maximum(j * halo_blocks - 1, 0), 0)),
            pl.BlockSpec((CONV_K, xbc), const),
            pl.BlockSpec((1, xbc), const),
            pl.BlockSpec((1, LANE), const),
            pl.BlockSpec((1, LANE), const),
            pl.BlockSpec((1, SSM_WIDTH), const),
            pl.BlockSpec((1, SSM_WIDTH), const),
            pl.BlockSpec((LANE, SSM_WIDTH), const),
        ],
        out_specs=pl.BlockSpec((None, c, SSM_WIDTH), lambda i, j: (i, j, 0)),
        out_shape=jax.ShapeDtypeStruct((b, t, SSM_WIDTH), BF16),
        scratch_shapes=[pltpu.VMEM((SSM_GROUPS, SSM_STATE, SSM_GROUP_WIDTH), F32)],
        compiler_params=_cparams(("parallel", "arbitrary")),
        name="ssd_mixer",
    )(proj, proj, conv_w, conv_b.reshape(1, xbc), a_log_row, dt_bias_row, d_skip_w, out_norm_w.reshape(1, SSM_WIDTH),
      expand)


def _out_proj_kernel(h_ref, og_ref, os_ref, wg_ref, ws_ref, o_ref):
    o_ref[...] = h_ref[...] + _dot(og_ref[...], wg_ref[...]) + _dot(os_ref[...], ws_ref[...])


def _out_proj(h, o_gdn, o_ssm, w, *, tm=512):
    n, d = h.shape
    assert GDN_WIDTH == SSM_WIDTH
    return pl.pallas_call(
        _out_proj_kernel,
        grid=(n // tm,),
        in_specs=[
            pl.BlockSpec((tm, d), lambda i: (i, 0)),
            pl.BlockSpec((tm, GDN_WIDTH), lambda i: (i, 0)),
            pl.BlockSpec((tm, SSM_WIDTH), lambda i: (i, 0)),
            pl.BlockSpec((GDN_WIDTH, d), lambda i: (0, 0)),
            pl.BlockSpec((SSM_WIDTH, d), lambda i: (1, 0)),
        ],
        out_specs=pl.BlockSpec((tm, d), lambda i: (i, 0)),
        out_shape=jax.ShapeDtypeStruct((n, d), F32),
        compiler_params=_cparams(("parallel",)),
        name="out_proj",
    )(h, o_gdn, o_ssm, w, w)


def _ple_kernel(h_ref, hn_ref, p_ref, wg_ref, wp_ref, pnw_ref, fnw_ref, o_ref, *, slabs):
    tm = h_ref.shape[0]
    for r in range(slabs):
        rows = pl.ds(r * (tm // slabs), tm // slabs)
        gate = _sigmoid(_dot(hn_ref[rows, :], wg_ref[...]))
        emb = _rms(_dot(p_ref[rows, :].astype(BF16), wp_ref[...])) * pnw_ref[...]
        o_ref[rows, :] = _rms(h_ref[rows, :] + gate * emb) * fnw_ref[...]


def _ple(h, hn, p, w_gate, w_proj, post_norm_w, final_norm_w, *, tm=512, slabs=2):
    n, d = h.shape
    pd = p.shape[1]
    const = lambda i: (0, 0)
    return pl.pallas_call(
        functools.partial(_ple_kernel, slabs=slabs),
        grid=(n // tm,),
        in_specs=[
            pl.BlockSpec((tm, d), lambda i: (i, 0)),
            pl.BlockSpec((tm, d), lambda i: (i, 0)),
            pl.BlockSpec((tm, pd), lambda i: (i, 0)),
            pl.BlockSpec((d, d), const),
            pl.BlockSpec((pd, d), const),
            pl.BlockSpec((1, d), const),
            pl.BlockSpec((1, d), const),
        ],
        out_specs=pl.BlockSpec((tm, d), lambda i: (i, 0)),
        out_shape=jax.ShapeDtypeStruct((n, d), F32),
        compiler_params=_cparams(("parallel",)),
        name="ple_final",
    )(h, hn, p, w_gate, w_proj, post_norm_w.reshape(1, d), final_norm_w.reshape(1, d))


def _lane_row(v):
    return jnp.zeros((1, LANE), F32).at[0, :v.shape[0]].set(v.astype(F32))


def kernel(x, p, ffn1_norm, ffn1_w_gate, ffn1_w_up, ffn1_w_down, mix_norm, w_in, gdn_conv_w, gdn_a_log, gdn_dt_bias, gdn_out_norm, ssm_conv_w, ssm_conv_b, ssm_a_log, ssm_dt_bias, ssm_d, ssm_out_norm, w_out, ffn2_norm, ffn2_w_gate, ffn2_w_up, ffn2_w_down, ple_norm, ple_w_gate, ple_w_proj, ple_post_norm, final_norm):
    bsz, seq, d = x.shape
    n = bsz * seq
    depth = p.shape[0]
    o_z = 4 * GDN_WIDTH + 2 * GDN_HEADS
    gdn_cols = 4 * GDN_WIDTH + LANE
    expand = (jnp.arange(LANE)[:, None] == (jnp.arange(SSM_WIDTH)[None, :] // SSM_HEAD_DIM)).astype(BF16)
    ffn_tm, ffn_tf = 512, 512

    h = x.reshape(n, d)
    for i in range(depth):
        wi_t = jnp.swapaxes(w_in[i], 0, 1)
        jobs = _cast_jobs((n // ffn_tm, ffn1_w_gate.shape[2] // ffn_tf),
                          [ffn2_w_gate[i], ffn2_w_up[i], ffn2_w_down[i], w_out[i], ple_w_gate[i], wi_t])
        h, hn, wg2, wu2, wd2, wo, wpg, wi_t = _ffn(
            h, ffn1_norm[i], ffn1_w_gate[i].astype(BF16), ffn1_w_up[i].astype(BF16), ffn1_w_down[i].astype(BF16),
            mix_norm[i], jobs, tm=ffn_tm, tf=ffn_tf)
        pad = jnp.zeros((LANE - SSM_HEADS, d), BF16)
        w_ssd_t = jnp.concatenate([wi_t[o_z:], pad], axis=0)
        proj_gdn = _matmul_nt(hn, wi_t, cols=gdn_cols, tn=gdn_cols // 3)
        proj_ssd = _matmul_nt(hn, w_ssd_t, tn=w_ssd_t.shape[0] // 3)
        o_gdn = _gdn_mixer(proj_gdn.reshape(bsz, seq, -1), gdn_conv_w[i], _lane_row(gdn_a_log[i]),
                           _lane_row(gdn_dt_bias[i]), gdn_out_norm[i])
        o_ssm = _ssd_mixer(proj_ssd.reshape(bsz, seq, -1), ssm_conv_w[i], ssm_conv_b[i],
                           _lane_row(ssm_a_log[i]), _lane_row(ssm_dt_bias[i]),
                           jnp.repeat(ssm_d[i].astype(F32), SSM_HEAD_DIM).reshape(1, SSM_WIDTH),
                           ssm_out_norm[i], expand)
        h = _out_proj(h, o_gdn.reshape(n, GDN_WIDTH), o_ssm.reshape(n, SSM_WIDTH), wo)
        h, hn = _ffn(h, ffn2_norm[i], wg2, wu2, wd2, ple_norm[i], tm=ffn_tm, tf=ffn_tf)
        assert i == depth - 1, "depth > 1 would need an un-normalised variant of the embedding kernel"
        h = _ple(h, hn, p[i].reshape(n, -1), wpg, ple_w_proj[i].astype(BF16), ple_post_norm[i], final_norm)
    return h.reshape(bsz, seq, d)
```

```python
import functools

import jax
import jax.numpy as jnp
from jax import lax
from jax.experimental import pallas as pl
from jax.experimental.pallas import tpu as pltpu

F32 = jnp.float32
BF16 = jnp.bfloat16
EPS = 1e-6

LANE = 128
CONV_K = 4
CONV_HALO = 8
MIX_CHUNK = 128
GDN_HEADS = 8
GDN_HEAD_DIM = 128
GDN_WIDTH = GDN_HEADS * GDN_HEAD_DIM
GDN_HEAD_GROUP = 8
SSM_HEADS = 16
SSM_HEAD_DIM = 64
SSM_WIDTH = SSM_HEADS * SSM_HEAD_DIM
SSM_GROUPS = 2
SSM_STATE = 128
SSM_GROUP_WIDTH = SSM_WIDTH // SSM_GROUPS
VMEM_LIMIT = 56 * 1024 * 1024


def _cparams(sem):
    return pltpu.CompilerParams(dimension_semantics=sem, vmem_limit_bytes=VMEM_LIMIT)


def _dot(a, b):
    return jnp.dot(a, b, preferred_element_type=F32)


def _dot_nt(a, b):
    return lax.dot_general(a, b, (((1,), (1,)), ((), ())), preferred_element_type=F32)


def _bdot(a, b):
    return _dot(a.astype(BF16), b.astype(BF16))


def _split3(a):
    a1 = a.astype(BF16)
    r1 = a - a1.astype(F32)
    a2 = r1.astype(BF16)
    a3 = (r1 - a2.astype(F32)).astype(BF16)
    return a1, a2, a3


def _dot_sel_lhs(sel, b):
    b1, b2, b3 = _split3(b)
    return _dot(sel, b1) + (_dot(sel, b2) + _dot(sel, b3))


def _dot_sel_rhs(a, sel):
    a1, a2, a3 = _split3(a)
    return _dot(a1, sel) + (_dot(a2, sel) + _dot(a3, sel))


def _sigmoid(x):
    return 1.0 / (1.0 + jnp.exp(-x))


def _silu(x):
    return x * _sigmoid(x)


def _softplus(x):
    return jnp.maximum(x, 0.0) + jnp.log1p(jnp.exp(-jnp.abs(x)))


def _rms(x):
    return x * lax.rsqrt(jnp.mean(x * x, axis=-1, keepdims=True) + EPS)


def _ffn_kernel(x_ref, nw_ref, wg_ref, wu_ref, wd_ref, nnw_ref, *rest, cast_once, slabs):
    n_cast = len(cast_once)
    cast_in = rest[:n_cast]
    o_ref, on_ref = rest[n_cast:n_cast + 2]
    cast_out = rest[n_cast + 2:2 * n_cast + 2]
    (xn_ref,) = rest[2 * n_cast + 2:]
    j = pl.program_id(1)
    last = pl.num_programs(1) - 1
    tm = x_ref.shape[0]
    slab_rows = [pl.ds(r * (tm // slabs), tm // slabs) for r in range(slabs)]

    for src, dst, once in zip(cast_in, cast_out, cast_once):
        if once:
            @pl.when(j == 0)
            def _(src=src, dst=dst):
                dst[...] = src[...].astype(BF16)
        else:
            dst[...] = src[...].astype(BF16)

    def step(rows, first, final):
        if first:
            base = x_ref[rows, :]
            xn = (_rms(base) * nw_ref[...]).astype(BF16)
            xn_ref[rows, :] = xn
        else:
            base = o_ref[rows, :]
            xn = xn_ref[rows, :]
        g = _dot(xn, wg_ref[...])
        u = _dot(xn, wu_ref[...])
        h = (0.5 * _silu(g) * u).astype(BF16)
        out = base + _dot(h, wd_ref[...].astype(BF16))
        o_ref[rows, :] = out
        if final:
            on_ref[rows, :] = (_rms(out) * nnw_ref[...]).astype(BF16)

    @pl.when(j == 0)
    def _():
        for rows in slab_rows:
            step(rows, True, False)

    @pl.when((j > 0) & (j < last))
    def _():
        step(slice(None), False, False)

    @pl.when(j == last)
    def _():
        for rows in slab_rows:
            step(rows, False, True)


def _ffn(x, norm_w, wg, wu, wd, next_norm_w, cast_jobs=(), *, tm=512, tf=512, slabs=2):
    n, d = x.shape
    f = wg.shape[1]
    grid = (n // tm, f // tf)
    assert grid[1] >= 2
    cast_specs = [pl.BlockSpec(blk, imap) for _, blk, imap, _ in cast_jobs]
    row_spec = pl.BlockSpec((tm, d), lambda i, j: (i, 0))
    vec_spec = pl.BlockSpec((1, d), lambda i, j: (0, 0))
    return pl.pallas_call(
        functools.partial(_ffn_kernel, cast_once=tuple(once for _, _, _, once in cast_jobs), slabs=slabs),
        grid=grid,
        in_specs=[
            row_spec,
            vec_spec,
            pl.BlockSpec((d, tf), lambda i, j: (0, j)),
            pl.BlockSpec((d, tf), lambda i, j: (0, j)),
            pl.BlockSpec((tf, d), lambda i, j: (j, 0)),
            vec_spec,
        ] + cast_specs,
        out_specs=[row_spec, row_spec] + cast_specs,
        out_shape=[jax.ShapeDtypeStruct((n, d), F32), jax.ShapeDtypeStruct((n, d), BF16)]
        + [jax.ShapeDtypeStruct(job[0].shape, BF16) for job in cast_jobs],
        scratch_shapes=[pltpu.VMEM((tm, d), BF16)],
        compiler_params=_cparams(("parallel", "arbitrary")),
        name="ffn",
    )(x, norm_w.reshape(1, d), wg, wu, wd, next_norm_w.reshape(1, d), *[job[0] for job in cast_jobs])


def _cast_jobs(grid, mats):
    gi, gj = grid
    jobs = []
    for a in mats:
        r, c = a.shape
        if c % LANE != 0:
            assert r % (gi * 8) == 0
            jobs.append((a, (r // gi, c), lambda i, j: (i, 0), True))
        elif r % (gi * 8) == 0 and c % (gj * LANE) == 0:
            jobs.append((a, (r // gi, c // gj), lambda i, j: (i, j), False))
        elif r % (gj * 8) == 0 and c % (gi * LANE) == 0:
            jobs.append((a, (r // gj, c // gi), lambda i, j: (j, i), False))
        else:
            nb = max(k for k in range(1, gj + 1) if c % (k * LANE) == 0)
            assert r % (gi * 8) == 0
            jobs.append((a, (r // gi, c // nb), lambda i, j, nb=nb: (i, jnp.minimum(j, nb - 1)), False))
    return jobs


def _matmul_nt_kernel(x_ref, w_ref, o_ref):
    o_ref[...] = _dot_nt(x_ref[...], w_ref[...])


def _matmul_nt(xn, wt, *, cols=None, tm=1024, tn):
    n, d = xn.shape
    cols = wt.shape[0] if cols is None else cols
    assert cols % tn == 0 and cols <= wt.shape[0]
    return pl.pallas_call(
        _matmul_nt_kernel,
        grid=(cols // tn, n // tm),
        in_specs=[
            pl.BlockSpec((tm, d), lambda j, i: (i, 0)),
            pl.BlockSpec((tn, d), lambda j, i: (j, 0)),
        ],
        out_specs=pl.BlockSpec((tm, tn), lambda j, i: (i, j)),
        out_shape=jax.ShapeDtypeStruct((n, cols), F32),
        compiler_params=_cparams(("parallel", "parallel")),
        name="in_proj",
    )(xn, wt)


def _causal_conv(halo, cur, w_ref, w_lo):
    c, width = cur.shape
    xx = jnp.concatenate([halo, cur], axis=0)
    y = w_ref[CONV_K - 1:CONV_K, w_lo:w_lo + width] * cur
    for j in range(CONV_K - 1):
        shifted = pltpu.roll(xx, CONV_K - 1 - j, axis=0)
        y = y + w_ref[j:j + 1, w_lo:w_lo + width] * shifted[CONV_HALO:CONV_HALO + c]
    return y


def _halo_rows(cur_ref, prev_ref, r0, cols, use_prev):
    if r0 == 0:
        return prev_ref[:, cols] * use_prev
    return cur_ref[r0 - CONV_HALO:r0, cols]


def _inv_unit_lower_many(mats, row, col):
    c = mats[0].shape[0]
    base = 16
    diag_blk = (row // base) == (col // base)
    eye = jnp.where(row == col, 1.0, 0.0)
    ps = [jnp.where(diag_blk, -a, 0.0) for a in mats]
    ts = [eye + p for p in ps]
    for _ in range(3):
        ps = [_bdot(p, p) for p in ps]
        ts = [t + _bdot(t, p) for t, p in zip(ts, ps)]
    b = 2 * base
    while b <= c:
        off = ((row // b) == (col // b)) & ((row // (b // 2)) != (col // (b // 2)))
        ets = [_bdot(jnp.where(off, a, 0.0), t) for a, t in zip(mats, ts)]
        ts = [t - _bdot(t, et) for t, et in zip(ts, ets)]
        b *= 2
    return ts


def _gdn_kernel(cur_ref, prev_ref, cw_ref, alog_ref, dtb_ref, onw_ref, o_ref, s_ref):
    t_idx = pl.program_id(1)

    @pl.when(t_idx == 0)
    def _():
        s_ref[...] = jnp.zeros_like(s_ref)

    c = MIX_CHUNK
    hd = GDN_HEAD_DIM
    use_prev = jnp.where(t_idx > 0, 1.0, 0.0)
    row = lax.broadcasted_iota(jnp.int32, (c, c), 0)
    col = lax.broadcasted_iota(jnp.int32, (c, c), 1)
    for r0 in range(0, cur_ref.shape[0], c):
        _gdn_chunk(r0, cur_ref, prev_ref, cw_ref, alog_ref, dtb_ref, onw_ref, o_ref, s_ref, use_prev, row, col)


def _gdn_chunk(r0, cur_ref, prev_ref, cw_ref, alog_ref, dtb_ref, onw_ref, o_ref, s_ref, use_prev, row, col):
    c = MIX_CHUNK
    hd = GDN_HEAD_DIM
    rows = slice(r0, r0 + c)
    causal = row >= col
    strict = row > col
    tril = jnp.where(causal, 1.0, 0.0).astype(BF16)

    small = cur_ref[rows, 4 * GDN_WIDTH:4 * GDN_WIDTH + LANE]
    g = -jnp.exp(alog_ref[...]) * _softplus(small + dtb_ref[...])
    beta = _sigmoid(small)
    gc = _dot_sel_lhs(tril, g)
    gc_t = gc.T
    g_last = gc[c - 1:c, :]
    e_gc = jnp.exp(gc)
    e_rest = jnp.exp(g_last - gc)
    g_tot = jnp.exp(g_last)

    def conv(lo):
        cols = slice(lo, lo + hd)
        halo = _halo_rows(cur_ref, prev_ref, r0, cols, use_prev)
        return _silu(_causal_conv(halo, cur_ref[rows, cols], cw_ref, lo))

    def l2norm(x):
        return x * lax.rsqrt(jnp.sum(x * x, axis=-1, keepdims=True) + EPS)

    for g0 in range(0, GDN_HEADS, GDN_HEAD_GROUP):
        heads = range(g0, g0 + GDN_HEAD_GROUP)
        q = {h: l2norm(conv(h * hd)) * (hd ** -0.5) for h in heads}
        k = {h: l2norm(conv(GDN_WIDTH + h * hd)) for h in heads}
        v = {h: conv(2 * GDN_WIDTH + h * hd) for h in heads}
        beta_c = {h: beta[:, GDN_HEADS + h:GDN_HEADS + h + 1] for h in heads}
        decay = {h: jnp.exp(jnp.where(causal, gc[:, h:h + 1] - gc_t[h:h + 1, :], -jnp.inf)) for h in heads}
        kb = {h: k[h] * beta_c[h] for h in heads}
        k_b = {h: k[h].astype(BF16) for h in heads}
        a = {h: jnp.where(strict, _dot_nt(kb[h].astype(BF16), k_b[h]) * decay[h], 0.0) for h in heads}
        qk = {h: (_dot_nt(q[h].astype(BF16), k_b[h]) * decay[h]).astype(BF16) for h in heads}
        t_inv = dict(zip(heads, _inv_unit_lower_many([a[h] for h in heads], row, col)))
        uw = {h: _bdot(t_inv[h], jnp.concatenate([v[h] * beta_c[h], kb[h] * e_gc[:, h:h + 1]], axis=1))
              for h in heads}
        s = {h: s_ref[h] for h in heads}
        wq = {h: jnp.concatenate([uw[h][:, hd:], q[h] * e_gc[:, h:h + 1]], axis=0) for h in heads}
        wq_s = {h: _bdot(wq[h], s[h]) for h in heads}
        v_new = {h: (uw[h][:, :hd] - wq_s[h][:c]).astype(BF16) for h in heads}
        o = {h: wq_s[h][c:] + _dot(qk[h], v_new[h]) for h in heads}
        k_dec_t = {h: (k[h] * e_rest[:, h:h + 1]).T.astype(BF16) for h in heads}
        for h in heads:
            s_ref[h] = s[h] * g_tot[:, h:h + 1] + _dot(k_dec_t[h], v_new[h])
        for h in heads:
            gate = cur_ref[rows, 3 * GDN_WIDTH + h * hd:3 * GDN_WIDTH + (h + 1) * hd]
            o_ref[rows, h * hd:(h + 1) * hd] = (_rms(o[h]) * onw_ref[...] * _silu(gate)).astype(o_ref.dtype)


def _gdn_mixer(proj, conv_w, a_log_row, dt_bias_row, out_norm_w, *, chunks_per_block=4):
    b, t, width = proj.shape
    c = chunks_per_block * MIX_CHUNK
    qkv = 3 * GDN_WIDTH
    halo_blocks = c // CONV_HALO
    return pl.pallas_call(
        _gdn_kernel,
        grid=(b, t // c),
        in_specs=[
            pl.BlockSpec((None, c, width), lambda i, j: (i, j, 0)),
            pl.BlockSpec((None, CONV_HALO, qkv), lambda i, j: (i, jnp.maximum(j * halo_blocks - 1, 0), 0)),
            pl.BlockSpec((CONV_K, qkv), lambda i, j: (0, 0)),
            pl.BlockSpec((1, LANE), lambda i, j: (0, 0)),
            pl.BlockSpec((1, LANE), lambda i, j: (0, 0)),
            pl.BlockSpec((1, GDN_HEAD_DIM), lambda i, j: (0, 0)),
        ],
        out_specs=pl.BlockSpec((None, c, GDN_WIDTH), lambda i, j: (i, j, 0)),
        out_shape=jax.ShapeDtypeStruct((b, t, GDN_WIDTH), BF16),
        scratch_shapes=[pltpu.VMEM((GDN_HEADS, GDN_HEAD_DIM, GDN_HEAD_DIM), F32)],
        compiler_params=_cparams(("parallel", "arbitrary")),
        name="gdn_mixer",
    )(proj, proj, conv_w, a_log_row, dt_bias_row, out_norm_w.reshape(1, GDN_HEAD_DIM))


def _ssd_kernel(cur_ref, prev_ref, cw_ref, cb_ref, alog_ref, dtb_ref, dskip_ref, onw_ref, expand_ref,
                o_ref, st_ref):
    t_idx = pl.program_id(1)

    @pl.when(t_idx == 0)
    def _():
        st_ref[...] = jnp.zeros_like(st_ref)

    c = MIX_CHUNK
    hp = SSM_HEAD_DIM
    gw = SSM_GROUP_WIDTH
    z_lo = 0
    x_lo = SSM_WIDTH
    small_lo = x_lo + SSM_WIDTH + 2 * SSM_GROUPS * SSM_STATE
    use_prev = jnp.where(t_idx > 0, 1.0, 0.0)
    row = lax.broadcasted_iota(jnp.int32, (c, c), 0)
    col = lax.broadcasted_iota(jnp.int32, (c, c), 1)
    causal = row >= col
    tril = jnp.where(causal, 1.0, 0.0).astype(BF16)
    lane = lax.broadcasted_iota(jnp.int32, (c, LANE), 1)
    first_head = lane < hp
    expand = expand_ref[...]

    for r0 in range(0, cur_ref.shape[0], c):
        rows = slice(r0, r0 + c)
        small = cur_ref[rows, small_lo:small_lo + LANE]
        dt = _softplus(small + dtb_ref[...])
        acs = _dot_sel_lhs(tril, dt * -jnp.exp(alog_ref[...]))
        acs_t = acs.T
        dt_w = _dot_sel_rhs(dt, expand)
        acs_w = _dot_sel_rhs(acs, expand)
        e_acs_w = jnp.exp(acs_w)
        e_rest_w = jnp.exp(acs_w[c - 1:c, :] - acs_w)

        def conv(lo, width, r0=r0, rows=rows):
            cols = slice(x_lo + lo, x_lo + lo + width)
            halo = _halo_rows(cur_ref, prev_ref, r0, cols, use_prev)
            return _silu(_causal_conv(halo, cur_ref[rows, cols], cw_ref, lo) + cb_ref[:, lo:lo + width])

        for g in range(SSM_GROUPS):
            bg = conv(SSM_WIDTH + g * SSM_STATE, SSM_STATE)
            cg = conv(SSM_WIDTH + (SSM_GROUPS + g) * SSM_STATE, SSM_STATE)
            bg_b = bg.astype(BF16)
            cg_b = cg.astype(BF16)
            cb = _dot_nt(cg_b, bg_b)
            cols = slice(g * gw, (g + 1) * gw)
            xs = conv(g * gw, gw)
            xc = xs * dt_w[:, cols]
            state = st_ref[g]
            y = _dot(cg_b, state.astype(BF16)) * e_acs_w[:, cols]
            chunk_state = _dot(bg.T.astype(BF16), (xc * e_rest_w[:, cols]).astype(BF16))
            st_ref[g] = state * e_acs_w[c - 1:c, cols] + chunk_state
            pieces = []
            for pr in range(gw // LANE):
                xp = xc[:, pr * LANE:(pr + 1) * LANE]
                lhs = []
                for r in range(2):
                    hidx = g * (SSM_HEADS // SSM_GROUPS) + 2 * pr + r
                    seg = jnp.exp(jnp.where(causal, acs[:, hidx:hidx + 1] - acs_t[hidx:hidx + 1, :], -jnp.inf))
                    lhs.append((cb * seg).astype(BF16))
                rhs = jnp.concatenate([jnp.where(first_head, xp, 0.0), jnp.where(first_head, 0.0, xp)], axis=0)
                pieces.append(_dot(jnp.concatenate(lhs, axis=1), rhs.astype(BF16)))
            y = y + jnp.concatenate(pieces, axis=1) + xs * dskip_ref[:, cols]
            y = y * _silu(cur_ref[rows, z_lo + g * gw:z_lo + (g + 1) * gw])
            o_ref[rows, cols] = (_rms(y) * onw_ref[:, cols]).astype(o_ref.dtype)


def _ssd_mixer(proj, conv_w, conv_b, a_log_row, dt_bias_row, d_skip_w, out_norm_w, expand, *, chunks_per_block=4):
    b, t, width = proj.shape
    c = chunks_per_block * MIX_CHUNK
    xbc = SSM_WIDTH + 2 * SSM_GROUPS * SSM_STATE
    halo_blocks = c // CONV_HALO
    const = lambda i, j: (0, 0)
    return pl.pallas_call(
        _ssd_kernel,
        grid=(b, t // c),
        in_specs=[
            pl.BlockSpec((None, c, width), lambda i, j: (i, j, 0)),
            pl.BlockSpec((None, CONV_HALO, width), lambda i, j: (i, jnp.maximum(j * halo_blocks - 1, 0), 0)),
            pl.BlockSpec((CONV_K, xbc), const),
            pl.BlockSpec((1, xbc), const),
            pl.BlockSpec((1, LANE), const),
            pl.BlockSpec((1, LANE), const),
            pl.BlockSpec((1, SSM_WIDTH), const),
            pl.BlockSpec((1, SSM_WIDTH), const),
            pl.BlockSpec((LANE, SSM_WIDTH), const),
        ],
        out_specs=pl.BlockSpec((None, c, SSM_WIDTH), lambda i, j: (i, j, 0)),
        out_shape=jax.ShapeDtypeStruct((b, t, SSM_WIDTH), BF16),
        scratch_shapes=[pltpu.VMEM((SSM_GROUPS, SSM_STATE, SSM_GROUP_WIDTH), F32)],
        compiler_params=_cparams(("parallel", "arbitrary")),
        name="ssd_mixer",
    )(proj, proj, conv_w, conv_b.reshape(1, xbc), a_log_row, dt_bias_row, d_skip_w, out_norm_w.reshape(1, SSM_WIDTH),
      expand)


def _out_proj_kernel(h_ref, og_ref, os_ref, wg_ref, ws_ref, o_ref):
    o_ref[...] = h_ref[...] + _dot(og_ref[...], wg_ref[...]) + _dot(os_ref[...], ws_ref[...])


def _out_proj(h, o_gdn, o_ssm, w, *, tm=512):
    n, d = h.shape
    assert GDN_WIDTH == SSM_WIDTH
    return pl.pallas_call(
        _out_proj_kernel,
        grid=(n // tm,),
        in_specs=[
            pl.BlockSpec((tm, d), lambda i: (i, 0)),
            pl.BlockSpec((tm, GDN_WIDTH), lambda i: (i, 0)),
            pl.BlockSpec((tm, SSM_WIDTH), lambda i: (i, 0)),
            pl.BlockSpec((GDN_WIDTH, d), lambda i: (0, 0)),
            pl.BlockSpec((SSM_WIDTH, d), lambda i: (1, 0)),
        ],
        out_specs=pl.BlockSpec((tm, d), lambda i: (i, 0)),
        out_shape=jax.ShapeDtypeStruct((n, d), F32),
        compiler_params=_cparams(("parallel",)),
        name="out_proj",
    )(h, o_gdn, o_ssm, w, w)


def _ple_kernel(h_ref, hn_ref, p_ref, wg_ref, wp_ref, pnw_ref, fnw_ref, o_ref, *, slabs):
    for lo, hi in zip(slabs[:-1], slabs[1:]):
        rows = pl.ds(lo, hi - lo)
        gate = _sigmoid(_dot(hn_ref[rows, :], wg_ref[...]))
        emb = _rms(_dot(p_ref[rows, :].astype(BF16), wp_ref[...])) * pnw_ref[...]
        o_ref[rows, :] = _rms(h_ref[rows, :] + gate * emb) * fnw_ref[...]


def _ple(h, hn, p, w_gate, w_proj, post_norm_w, final_norm_w, *, tm=512, slabs=(0, 256, 512)):
    n, d = h.shape
    pd = p.shape[1]
    const = lambda i: (0, 0)
    return pl.pallas_call(
        functools.partial(_ple_kernel, slabs=slabs),
        grid=(n // tm,),
        in_specs=[
            pl.BlockSpec((tm, d), lambda i: (i, 0)),
            pl.BlockSpec((tm, d), lambda i: (i, 0)),
            pl.BlockSpec((tm, pd), lambda i: (i, 0)),
            pl.BlockSpec((d, d), const),
            pl.BlockSpec((pd, d), const),
            pl.BlockSpec((1, d), const),
            pl.BlockSpec((1, d), const),
        ],
        out_specs=pl.BlockSpec((tm, d), lambda i: (i, 0)),
        out_shape=jax.ShapeDtypeStruct((n, d), F32),
        compiler_params=_cparams(("parallel",)),
        name="ple_final",
    )(h, hn, p, w_gate, w_proj, post_norm_w.reshape(1, d), final_norm_w.reshape(1, d))


def _lane_row(v):
    return jnp.zeros((1, LANE), F32).at[0, :v.shape[0]].set(v.astype(F32))


def kernel(x, p, ffn1_norm, ffn1_w_gate, ffn1_w_up, ffn1_w_down, mix_norm, w_in, gdn_conv_w, gdn_a_log, gdn_dt_bias, gdn_out_norm, ssm_conv_w, ssm_conv_b, ssm_a_log, ssm_dt_bias, ssm_d, ssm_out_norm, w_out, ffn2_norm, ffn2_w_gate, ffn2_w_up, ffn2_w_down, ple_norm, ple_w_gate, ple_w_proj, ple_post_norm, final_norm):
    bsz, seq, d = x.shape
    n = bsz * seq
    depth = p.shape[0]
    o_z = 4 * GDN_WIDTH + 2 * GDN_HEADS
    gdn_cols = 4 * GDN_WIDTH + LANE
    expand = (jnp.arange(LANE)[:, None] == (jnp.arange(SSM_WIDTH)[None, :] // SSM_HEAD_DIM)).astype(BF16)
    ffn_tm, ffn_tf = 512, 512

    h = x.reshape(n, d)
    for i in range(depth):
        wi_t = jnp.swapaxes(w_in[i], 0, 1)
        jobs = _cast_jobs((n // ffn_tm, ffn1_w_gate.shape[2] // ffn_tf),
                          [ffn2_w_gate[i], ffn2_w_up[i], ffn2_w_down[i], w_out[i], ple_w_gate[i], wi_t])
        h, hn, wg2, wu2, wd2, wo, wpg, wi_t = _ffn(
            h, ffn1_norm[i], ffn1_w_gate[i].astype(BF16), ffn1_w_up[i].astype(BF16), ffn1_w_down[i],
            mix_norm[i], jobs, tm=ffn_tm, tf=ffn_tf)
        pad = jnp.zeros((LANE - SSM_HEADS, d), BF16)
        w_ssd_t = jnp.concatenate([wi_t[o_z:], pad], axis=0)
        proj_gdn = _matmul_nt(hn, wi_t, cols=gdn_cols, tn=gdn_cols // 3)
        proj_ssd = _matmul_nt(hn, w_ssd_t, tn=w_ssd_t.shape[0] // 3)
        o_gdn = _gdn_mixer(proj_gdn.reshape(bsz, seq, -1), gdn_conv_w[i], _lane_row(gdn_a_log[i]),
                           _lane_row(gdn_dt_bias[i]), gdn_out_norm[i])
        o_ssm = _ssd_mixer(proj_ssd.reshape(bsz, seq, -1), ssm_conv_w[i], ssm_conv_b[i],
                           _lane_row(ssm_a_log[i]), _lane_row(ssm_dt_bias[i]),
                           jnp.repeat(ssm_d[i].astype(F32), SSM_HEAD_DIM).reshape(1, SSM_WIDTH),
                           ssm_out_norm[i], expand)
        h = _out_proj(h, o_gdn.reshape(n, GDN_WIDTH), o_ssm.reshape(n, SSM_WIDTH), wo)
        h, hn = _ffn(h, ffn2_norm[i], wg2, wu2, wd2, ple_norm[i], tm=ffn_tm, tf=ffn_tf)
        assert i == depth - 1, "depth > 1 would need an un-normalised variant of the embedding kernel"
        h = _ple(h, hn, p[i].reshape(n, -1), wpg, ple_w_proj[i].astype(BF16), ple_post_norm[i], final_norm)
    return h.reshape(bsz, seq, d)
```

```python
import functools

import jax
import jax.numpy as jnp
from jax import lax
from jax.experimental import pallas as pl
from jax.experimental.pallas import tpu as pltpu

F32 = jnp.float32
BF16 = jnp.bfloat16
EPS = 1e-6

LANE = 128
CONV_K = 4
CONV_HALO = 8
MIX_CHUNK = 128
GDN_HEADS = 8
GDN_HEAD_DIM = 128
GDN_WIDTH = GDN_HEADS * GDN_HEAD_DIM
GDN_HEAD_GROUP = 8
SSM_HEADS = 16
SSM_HEAD_DIM = 64
SSM_WIDTH = SSM_HEADS * SSM_HEAD_DIM
SSM_GROUPS = 2
SSM_STATE = 128
SSM_GROUP_WIDTH = SSM_WIDTH // SSM_GROUPS
VMEM_LIMIT = 60 * 1024 * 1024


def _cparams(sem):
    return pltpu.CompilerParams(dimension_semantics=sem, vmem_limit_bytes=VMEM_LIMIT)


def _dot(a, b):
    return jnp.dot(a, b, preferred_element_type=F32)


def _dot_nt(a, b):
    return lax.dot_general(a, b, (((1,), (1,)), ((), ())), preferred_element_type=F32)


def _bdot(a, b):
    return _dot(a.astype(BF16), b.astype(BF16))


def _split3(a):
    a1 = a.astype(BF16)
    r1 = a - a1.astype(F32)
    a2 = r1.astype(BF16)
    a3 = (r1 - a2.astype(F32)).astype(BF16)
    return a1, a2, a3


def _dot_sel_lhs(sel, b):
    b1, b2, b3 = _split3(b)
    return _dot(sel, b1) + (_dot(sel, b2) + _dot(sel, b3))


def _dot_sel_rhs(a, sel):
    a1, a2, a3 = _split3(a)
    return _dot(a1, sel) + (_dot(a2, sel) + _dot(a3, sel))


def _sigmoid(x):
    return 1.0 / (1.0 + jnp.exp(-x))


def _silu(x):
    return x * _sigmoid(x)


def _softplus(x):
    return jnp.maximum(x, 0.0) + jnp.log1p(jnp.exp(-jnp.abs(x)))


def _rms(x):
    return x * lax.rsqrt(jnp.mean(x * x, axis=-1, keepdims=True) + EPS)


def _ffn_kernel(x_ref, nw_ref, wg_ref, wu_ref, wd_ref, nnw_ref, *rest, cast_once, slabs):
    n_cast = len(cast_once)
    cast_in = rest[:n_cast]
    o_ref, on_ref = rest[n_cast:n_cast + 2]
    cast_out = rest[n_cast + 2:2 * n_cast + 2]
    (xn_ref,) = rest[2 * n_cast + 2:]
    j = pl.program_id(1)
    last = pl.num_programs(1) - 1
    tm = x_ref.shape[0]
    slab_rows = [pl.ds(r * (tm // slabs), tm // slabs) for r in range(slabs)]

    for src, dst, once in zip(cast_in, cast_out, cast_once):
        if once:
            @pl.when(j == 0)
            def _(src=src, dst=dst):
                dst[...] = src[...].astype(BF16)
        else:
            dst[...] = src[...].astype(BF16)

    def step(rows, first, final):
        if first:
            base = x_ref[rows, :]
            xn = (_rms(base) * nw_ref[...]).astype(BF16)
            xn_ref[rows, :] = xn
        else:
            base = o_ref[rows, :]
            xn = xn_ref[rows, :]
        g = _dot(xn, wg_ref[...])
        u = _dot(xn, wu_ref[...])
        h = (0.5 * _silu(g) * u).astype(BF16)
        out = base + _dot(h, wd_ref[...])
        o_ref[rows, :] = out
        if final:
            on_ref[rows, :] = (_rms(out) * nnw_ref[...]).astype(BF16)

    @pl.when(j == 0)
    def _():
        for rows in slab_rows:
            step(rows, True, False)

    @pl.when((j > 0) & (j < last))
    def _():
        step(slice(None), False, False)

    @pl.when(j == last)
    def _():
        for rows in slab_rows:
            step(rows, False, True)


def _ffn(x, norm_w, wg, wu, wd, next_norm_w, cast_jobs=(), *, tm, tf, slabs=2, x_single_buffer=False):
    n, d = x.shape
    f = wg.shape[1]
    grid = (n // tm, f // tf)
    assert grid[1] >= 2
    cast_specs = [pl.BlockSpec(blk, imap) for _, blk, imap, _ in cast_jobs]
    row_spec = pl.BlockSpec((tm, d), lambda i, j: (i, 0))
    vec_spec = pl.BlockSpec((1, d), lambda i, j: (0, 0))
    x_spec = pl.BlockSpec((tm, d), lambda i, j: (i, 0), pipeline_mode=pl.Buffered(1)) if x_single_buffer else row_spec
    return pl.pallas_call(
        functools.partial(_ffn_kernel, cast_once=tuple(once for _, _, _, once in cast_jobs), slabs=slabs),
        grid=grid,
        in_specs=[
            x_spec,
            vec_spec,
            pl.BlockSpec((d, tf), lambda i, j: (0, j)),
            pl.BlockSpec((d, tf), lambda i, j: (0, j)),
            pl.BlockSpec((tf, d), lambda i, j: (j, 0)),
            vec_spec,
        ] + cast_specs,
        out_specs=[row_spec, row_spec] + cast_specs,
        out_shape=[jax.ShapeDtypeStruct((n, d), F32), jax.ShapeDtypeStruct((n, d), BF16)]
        + [jax.ShapeDtypeStruct(job[0].shape, BF16) for job in cast_jobs],
        scratch_shapes=[pltpu.VMEM((tm, d), BF16)],
        compiler_params=_cparams(("parallel", "arbitrary")),
        name="ffn",
    )(x, norm_w.reshape(1, d), wg, wu, wd, next_norm_w.reshape(1, d), *[job[0] for job in cast_jobs])


def _cast_jobs(grid, mats):
    gi, gj = grid
    jobs = []
    for a in mats:
        r, c = a.shape
        if c % LANE != 0:
            assert r % (gi * 8) == 0
            jobs.append((a, (r // gi, c), lambda i, j: (i, 0), True))
        elif r % (gi * 8) == 0 and c % (gj * LANE) == 0:
            jobs.append((a, (r // gi, c // gj), lambda i, j: (i, j), False))
        elif r % (gj * 8) == 0 and c % (gi * LANE) == 0:
            jobs.append((a, (r // gj, c // gi), lambda i, j: (j, i), False))
        else:
            nb = max(k for k in range(1, gj + 1) if c % (k * LANE) == 0)
            assert r % (gi * 8) == 0
            jobs.append((a, (r // gi, c // nb), lambda i, j, nb=nb: (i, jnp.minimum(j, nb - 1)), False))
    return jobs


def _matmul_nt_kernel(x_ref, w_ref, o_ref):
    o_ref[...] = _dot_nt(x_ref[...], w_ref[...])


def _matmul_nt(xn, wt, *, cols=None, tm=1024, tn):
    n, d = xn.shape
    cols = wt.shape[0] if cols is None else cols
    assert cols % tn == 0 and cols <= wt.shape[0]
    return pl.pallas_call(
        _matmul_nt_kernel,
        grid=(cols // tn, n // tm),
        in_specs=[
            pl.BlockSpec((tm, d), lambda j, i: (i, 0)),
            pl.BlockSpec((tn, d), lambda j, i: (j, 0)),
        ],
        out_specs=pl.BlockSpec((tm, tn), lambda j, i: (i, j)),
        out_shape=jax.ShapeDtypeStruct((n, cols), F32),
        compiler_params=_cparams(("parallel", "parallel")),
        name="in_proj",
    )(xn, wt)


def _causal_conv(halo, cur, w_ref, w_lo):
    c, width = cur.shape
    xx = jnp.concatenate([halo, cur], axis=0)
    y = w_ref[CONV_K - 1:CONV_K, w_lo:w_lo + width] * cur
    for j in range(CONV_K - 1):
        shifted = pltpu.roll(xx, CONV_K - 1 - j, axis=0)
        y = y + w_ref[j:j + 1, w_lo:w_lo + width] * shifted[CONV_HALO:CONV_HALO + c]
    return y


def _halo_rows(cur_ref, prev_ref, r0, cols, use_prev):
    if r0 == 0:
        return prev_ref[:, cols] * use_prev
    return cur_ref[r0 - CONV_HALO:r0, cols]


def _inv_unit_lower_many(mats, row, col):
    c = mats[0].shape[0]
    base = 16
    diag_blk = (row // base) == (col // base)
    eye = jnp.where(row == col, 1.0, 0.0)
    ps = [jnp.where(diag_blk, -a, 0.0) for a in mats]
    ts = [eye + p for p in ps]
    for _ in range(3):
        ps = [_bdot(p, p) for p in ps]
        ts = [t + _bdot(t, p) for t, p in zip(ts, ps)]
    b = 2 * base
    while b <= c:
        off = ((row // b) == (col // b)) & ((row // (b // 2)) != (col // (b // 2)))
        ets = [_bdot(jnp.where(off, a, 0.0), t) for a, t in zip(mats, ts)]
        ts = [t - _bdot(t, et) for t, et in zip(ts, ets)]
        b *= 2
    return ts


def _gdn_kernel(cur_ref, prev_ref, cw_ref, alog_ref, dtb_ref, onw_ref, o_ref, s_ref):
    t_idx = pl.program_id(1)

    @pl.when(t_idx == 0)
    def _():
        s_ref[...] = jnp.zeros_like(s_ref)

    c = MIX_CHUNK
    hd = GDN_HEAD_DIM
    use_prev = jnp.where(t_idx > 0, 1.0, 0.0)
    row = lax.broadcasted_iota(jnp.int32, (c, c), 0)
    col = lax.broadcasted_iota(jnp.int32, (c, c), 1)
    for r0 in range(0, cur_ref.shape[0], c):
        _gdn_chunk(r0, cur_ref, prev_ref, cw_ref, alog_ref, dtb_ref, onw_ref, o_ref, s_ref, use_prev, row, col)


def _gdn_chunk(r0, cur_ref, prev_ref, cw_ref, alog_ref, dtb_ref, onw_ref, o_ref, s_ref, use_prev, row, col):
    c = MIX_CHUNK
    hd = GDN_HEAD_DIM
    rows = slice(r0, r0 + c)
    causal = row >= col
    strict = row > col
    tril = jnp.where(causal, 1.0, 0.0).astype(BF16)

    small = cur_ref[rows, 4 * GDN_WIDTH:4 * GDN_WIDTH + LANE]
    g = -jnp.exp(alog_ref[...]) * _softplus(small + dtb_ref[...])
    beta = _sigmoid(small)
    gc = _dot_sel_lhs(tril, g)
    gc_t = gc.T
    g_last = gc[c - 1:c, :]
    e_gc = jnp.exp(gc)
    e_rest = jnp.exp(g_last - gc)
    g_tot = jnp.exp(g_last)

    def conv(lo):
        cols = slice(lo, lo + hd)
        halo = _halo_rows(cur_ref, prev_ref, r0, cols, use_prev)
        return _silu(_causal_conv(halo, cur_ref[rows, cols], cw_ref, lo))

    def l2norm(x):
        return x * lax.rsqrt(jnp.sum(x * x, axis=-1, keepdims=True) + EPS)

    for g0 in range(0, GDN_HEADS, GDN_HEAD_GROUP):
        heads = range(g0, g0 + GDN_HEAD_GROUP)
        q = {h: l2norm(conv(h * hd)) * (hd ** -0.5) for h in heads}
        k = {h: l2norm(conv(GDN_WIDTH + h * hd)) for h in heads}
        v = {h: conv(2 * GDN_WIDTH + h * hd) for h in heads}
        beta_c = {h: beta[:, GDN_HEADS + h:GDN_HEADS + h + 1] for h in heads}
        decay = {h: jnp.exp(jnp.where(causal, gc[:, h:h + 1] - gc_t[h:h + 1, :], -jnp.inf)) for h in heads}
        kb = {h: k[h] * beta_c[h] for h in heads}
        k_b = {h: k[h].astype(BF16) for h in heads}
        a = {h: jnp.where(strict, _dot_nt(kb[h].astype(BF16), k_b[h]) * decay[h], 0.0) for h in heads}
        qk = {h: (_dot_nt(q[h].astype(BF16), k_b[h]) * decay[h]).astype(BF16) for h in heads}
        t_inv = dict(zip(heads, _inv_unit_lower_many([a[h] for h in heads], row, col)))
        uw = {h: _bdot(t_inv[h], jnp.concatenate([v[h] * beta_c[h], kb[h] * e_gc[:, h:h + 1]], axis=1))
              for h in heads}
        s = {h: s_ref[h] for h in heads}
        wq = {h: jnp.concatenate([uw[h][:, hd:], q[h] * e_gc[:, h:h + 1]], axis=0) for h in heads}
        wq_s = {h: _bdot(wq[h], s[h]) for h in heads}
        v_new = {h: (uw[h][:, :hd] - wq_s[h][:c]).astype(BF16) for h in heads}
        o = {h: wq_s[h][c:] + _dot(qk[h], v_new[h]) for h in heads}
        k_dec_t = {h: (k[h] * e_rest[:, h:h + 1]).T.astype(BF16) for h in heads}
        for h in heads:
            s_ref[h] = s[h] * g_tot[:, h:h + 1] + _dot(k_dec_t[h], v_new[h])
        for h in heads:
            gate = cur_ref[rows, 3 * GDN_WIDTH + h * hd:3 * GDN_WIDTH + (h + 1) * hd]
            o_ref[rows, h * hd:(h + 1) * hd] = (_rms(o[h]) * onw_ref[...] * _silu(gate)).astype(o_ref.dtype)


def _gdn_mixer(proj, conv_w, a_log_row, dt_bias_row, out_norm_w, *, chunks_per_block=2):
    b, t, width = proj.shape
    c = chunks_per_block * MIX_CHUNK
    qkv = 3 * GDN_WIDTH
    halo_blocks = c // CONV_HALO
    return pl.pallas_call(
        _gdn_kernel,
        grid=(b, t // c),
        in_specs=[
            pl.BlockSpec((None, c, width), lambda i, j: (i, j, 0)),
            pl.BlockSpec((None, CONV_HALO, qkv), lambda i, j: (i, jnp.maximum(j * halo_blocks - 1, 0), 0)),
            pl.BlockSpec((CONV_K, qkv), lambda i, j: (0, 0)),
            pl.BlockSpec((1, LANE), lambda i, j: (0, 0)),
            pl.BlockSpec((1, LANE), lambda i, j: (0, 0)),
            pl.BlockSpec((1, GDN_HEAD_DIM), lambda i, j: (0, 0)),
        ],
        out_specs=pl.BlockSpec((None, c, GDN_WIDTH), lambda i, j: (i, j, 0)),
        out_shape=jax.ShapeDtypeStruct((b, t, GDN_WIDTH), BF16),
        scratch_shapes=[pltpu.VMEM((GDN_HEADS, GDN_HEAD_DIM, GDN_HEAD_DIM), F32)],
        compiler_params=_cparams(("parallel", "arbitrary")),
        name="gdn_mixer",
    )(proj, proj, conv_w, a_log_row, dt_bias_row, out_norm_w.reshape(1, GDN_HEAD_DIM))


def _ssd_kernel(cur_ref, prev_ref, cw_ref, cb_ref, alog_ref, dtb_ref, dskip_ref, onw_ref, expand_ref,
                o_ref, st_ref):
    t_idx = pl.program_id(1)

    @pl.when(t_idx == 0)
    def _():
        st_ref[...] = jnp.zeros_like(st_ref)

    c = MIX_CHUNK
    hp = SSM_HEAD_DIM
    gw = SSM_GROUP_WIDTH
    z_lo = 0
    x_lo = SSM_WIDTH
    small_lo = x_lo + SSM_WIDTH + 2 * SSM_GROUPS * SSM_STATE
    use_prev = jnp.where(t_idx > 0, 1.0, 0.0)
    row = lax.broadcasted_iota(jnp.int32, (c, c), 0)
    col = lax.broadcasted_iota(jnp.int32, (c, c), 1)
    causal = row >= col
    tril = jnp.where(causal, 1.0, 0.0).astype(BF16)
    lane = lax.broadcasted_iota(jnp.int32, (c, LANE), 1)
    first_head = lane < hp
    expand = expand_ref[...]

    for r0 in range(0, cur_ref.shape[0], c):
        rows = slice(r0, r0 + c)
        small = cur_ref[rows, small_lo:small_lo + LANE]
        dt = _softplus(small + dtb_ref[...])
        acs = _dot_sel_lhs(tril, dt * -jnp.exp(alog_ref[...]))
        acs_t = acs.T
        dt_w = _dot_sel_rhs(dt, expand)
        acs_w = _dot_sel_rhs(acs, expand)
        e_acs_w = jnp.exp(acs_w)
        e_rest_w = jnp.exp(acs_w[c - 1:c, :] - acs_w)

        def conv(lo, width, r0=r0, rows=rows):
            cols = slice(x_lo + lo, x_lo + lo + width)
            halo = _halo_rows(cur_ref, prev_ref, r0, cols, use_prev)
            return _silu(_causal_conv(halo, cur_ref[rows, cols], cw_ref, lo) + cb_ref[:, lo:lo + width])

        for g in range(SSM_GROUPS):
            bg = conv(SSM_WIDTH + g * SSM_STATE, SSM_STATE)
            cg = conv(SSM_WIDTH + (SSM_GROUPS + g) * SSM_STATE, SSM_STATE)
            bg_b = bg.astype(BF16)
            cg_b = cg.astype(BF16)
            cb = _dot_nt(cg_b, bg_b)
            cols = slice(g * gw, (g + 1) * gw)
            xs = conv(g * gw, gw)
            xc = xs * dt_w[:, cols]
            state = st_ref[g]
            y = _dot(cg_b, state.astype(BF16)) * e_acs_w[:, cols]
            chunk_state = _dot(bg.T.astype(BF16), (xc * e_rest_w[:, cols]).astype(BF16))
            st_ref[g] = state * e_acs_w[c - 1:c, cols] + chunk_state
            pieces = []
            for pr in range(gw // LANE):
                xp = xc[:, pr * LANE:(pr + 1) * LANE]
                lhs = []
                for r in range(2):
                    hidx = g * (SSM_HEADS // SSM_GROUPS) + 2 * pr + r
                    seg = jnp.exp(jnp.where(causal, acs[:, hidx:hidx + 1] - acs_t[hidx:hidx + 1, :], -jnp.inf))
                    lhs.append((cb * seg).astype(BF16))
                rhs = jnp.concatenate([jnp.where(first_head, xp, 0.0), jnp.where(first_head, 0.0, xp)], axis=0)
                pieces.append(_dot(jnp.concatenate(lhs, axis=1), rhs.astype(BF16)))
            y = y + jnp.concatenate(pieces, axis=1) + xs * dskip_ref[:, cols]
            y = y * _silu(cur_ref[rows, z_lo + g * gw:z_lo + (g + 1) * gw])
            o_ref[rows, cols] = (_rms(y) * onw_ref[:, cols]).astype(o_ref.dtype)


def _ssd_mixer(proj, conv_w, conv_b, a_log_row, dt_bias_row, d_skip_w, out_norm_w, expand, *, chunks_per_block=2):
    b, t, width = proj.shape
    c = chunks_per_block * MIX_CHUNK
    xbc = SSM_WIDTH + 2 * SSM_GROUPS * SSM_STATE
    halo_blocks = c // CONV_HALO
    const = lambda i, j: (0, 0)
    return pl.pallas_call(
        _ssd_kernel,
        grid=(b, t // c),
        in_specs=[
            pl.BlockSpec((None, c, width), lambda i, j: (i, j, 0)),
            pl.BlockSpec((None, CONV_HALO, width), lambda i, j: (i, jnp.maximum(j * halo_blocks - 1, 0), 0)),
            pl.BlockSpec((CONV_K, xbc), const),
            pl.BlockSpec((1, xbc), const),
            pl.BlockSpec((1, LANE), const),
            pl.BlockSpec((1, LANE), const),
            pl.BlockSpec((1, SSM_WIDTH), const),
            pl.BlockSpec((1, SSM_WIDTH), const),
            pl.BlockSpec((LANE, SSM_WIDTH), const),
        ],
        out_specs=pl.BlockSpec((None, c, SSM_WIDTH), lambda i, j: (i, j, 0)),
        out_shape=jax.ShapeDtypeStruct((b, t, SSM_WIDTH), BF16),
        scratch_shapes=[pltpu.VMEM((SSM_GROUPS, SSM_STATE, SSM_GROUP_WIDTH), F32)],
        compiler_params=_cparams(("parallel", "arbitrary")),
        name="ssd_mixer",
    )(proj, proj, conv_w, conv_b.reshape(1, xbc), a_log_row, dt_bias_row, d_skip_w, out_norm_w.reshape(1, SSM_WIDTH),
      expand)


def _out_proj_kernel(h_ref, og_ref, os_ref, wg_ref, ws_ref, o_ref):
    o_ref[...] = h_ref[...] + _dot(og_ref[...], wg_ref[...]) + _dot(os_ref[...], ws_ref[...])


def _out_proj(h, o_gdn, o_ssm, w, *, tm=512):
    n, d = h.shape
    assert GDN_WIDTH == SSM_WIDTH
    return pl.pallas_call(
        _out_proj_kernel,
        grid=(n // tm,),
        in_specs=[
            pl.BlockSpec((tm, d), lambda i: (i, 0)),
            pl.BlockSpec((tm, GDN_WIDTH), lambda i: (i, 0)),
            pl.BlockSpec((tm, SSM_WIDTH), lambda i: (i, 0)),
            pl.BlockSpec((GDN_WIDTH, d), lambda i: (0, 0)),
            pl.BlockSpec((SSM_WIDTH, d), lambda i: (1, 0)),
        ],
        out_specs=pl.BlockSpec((tm, d), lambda i: (i, 0)),
        out_shape=jax.ShapeDtypeStruct((n, d), F32),
        compiler_params=_cparams(("parallel",)),
        name="out_proj",
    )(h, o_gdn, o_ssm, w, w)


def _ple_kernel(h_ref, hn_ref, p_ref, wg_ref, wp_ref, pnw_ref, fnw_ref, o_ref, *, slabs):
    for lo, hi in zip(slabs[:-1], slabs[1:]):
        rows = pl.ds(lo, hi - lo)
        gate = _sigmoid(_dot(hn_ref[rows, :], wg_ref[...]))
        emb = _rms(_dot(p_ref[rows, :].astype(BF16), wp_ref[...])) * pnw_ref[...]
        o_ref[rows, :] = _rms(h_ref[rows, :] + gate * emb) * fnw_ref[...]


def _ple(h, hn, p, w_gate, w_proj, post_norm_w, final_norm_w, *, tm=512, slabs=(0, 256, 512)):
    n, d = h.shape
    pd = p.shape[1]
    const = lambda i: (0, 0)
    return pl.pallas_call(
        functools.partial(_ple_kernel, slabs=slabs),
        grid=(n // tm,),
        in_specs=[
            pl.BlockSpec((tm, d), lambda i: (i, 0)),
            pl.BlockSpec((tm, d), lambda i: (i, 0)),
            pl.BlockSpec((tm, pd), lambda i: (i, 0)),
            pl.BlockSpec((d, d), const),
            pl.BlockSpec((pd, d), const),
            pl.BlockSpec((1, d), const),
            pl.BlockSpec((1, d), const),
        ],
        out_specs=pl.BlockSpec((tm, d), lambda i: (i, 0)),
        out_shape=jax.ShapeDtypeStruct((n, d), F32),
        compiler_params=_cparams(("parallel",)),
        name="ple_final",
    )(h, hn, p, w_gate, w_proj, post_norm_w.reshape(1, d), final_norm_w.reshape(1, d))


def _lane_row(v):
    return jnp.zeros((1, LANE), F32).at[0, :v.shape[0]].set(v.astype(F32))


def kernel(x, p, ffn1_norm, ffn1_w_gate, ffn1_w_up, ffn1_w_down, mix_norm, w_in, gdn_conv_w, gdn_a_log, gdn_dt_bias, gdn_out_norm, ssm_conv_w, ssm_conv_b, ssm_a_log, ssm_dt_bias, ssm_d, ssm_out_norm, w_out, ffn2_norm, ffn2_w_gate, ffn2_w_up, ffn2_w_down, ple_norm, ple_w_gate, ple_w_proj, ple_post_norm, final_norm):
    bsz, seq, d = x.shape
    n = bsz * seq
    depth = p.shape[0]
    o_z = 4 * GDN_WIDTH + 2 * GDN_HEADS
    gdn_cols = 4 * GDN_WIDTH + LANE
    expand = (jnp.arange(LANE)[:, None] == (jnp.arange(SSM_WIDTH)[None, :] // SSM_HEAD_DIM)).astype(BF16)
    ffn_tm, ffn1_tf, ffn2_tf = 1024, 256, 512

    h = x.reshape(n, d)
    for i in range(depth):
        wi_t = jnp.swapaxes(w_in[i], 0, 1)
        jobs = _cast_jobs((n // ffn_tm, ffn1_w_gate.shape[2] // ffn1_tf),
                          [ffn2_w_gate[i], ffn2_w_up[i], ffn2_w_down[i], w_out[i], ple_w_gate[i], wi_t])
        h, hn, wg2, wu2, wd2, wo, wpg, wi_t = _ffn(
            h, ffn1_norm[i], ffn1_w_gate[i].astype(BF16), ffn1_w_up[i].astype(BF16), ffn1_w_down[i].astype(BF16),
            mix_norm[i], jobs, tm=ffn_tm, tf=ffn1_tf)
        pad = jnp.zeros((LANE - SSM_HEADS, d), BF16)
        w_ssd_t = jnp.concatenate([wi_t[o_z:], pad], axis=0)
        proj_gdn = _matmul_nt(hn, wi_t, cols=gdn_cols, tn=gdn_cols // 3)
        proj_ssd = _matmul_nt(hn, w_ssd_t, tn=w_ssd_t.shape[0] // 3)
        o_gdn = _gdn_mixer(proj_gdn.reshape(bsz, seq, -1), gdn_conv_w[i], _lane_row(gdn_a_log[i]),
                           _lane_row(gdn_dt_bias[i]), gdn_out_norm[i])
        o_ssm = _ssd_mixer(proj_ssd.reshape(bsz, seq, -1), ssm_conv_w[i], ssm_conv_b[i],
                           _lane_row(ssm_a_log[i]), _lane_row(ssm_dt_bias[i]),
                           jnp.repeat(ssm_d[i].astype(F32), SSM_HEAD_DIM).reshape(1, SSM_WIDTH),
                           ssm_out_norm[i], expand)
        h = _out_proj(h, o_gdn.reshape(n, GDN_WIDTH), o_ssm.reshape(n, SSM_WIDTH), wo)
        h, hn = _ffn(h, ffn2_norm[i], wg2, wu2, wd2, ple_norm[i], tm=ffn_tm, tf=ffn2_tf, x_single_buffer=True)
        assert i == depth - 1, "depth > 1 would need an un-normalised variant of the embedding kernel"
        h = _ple(h, hn, p[i].reshape(n, -1), wpg, ple_w_proj[i].astype(BF16), ple_post_norm[i], final_norm)
    return h.reshape(bsz, seq, d)
```

```python
import functools

import jax
import jax.numpy as jnp
from jax import lax
from jax.experimental import pallas as pl
from jax.experimental.pallas import tpu as pltpu

F32 = jnp.float32
BF16 = jnp.bfloat16
EPS = 1e-6

LANE = 128
CONV_K = 4
CONV_HALO = 8
MIX_CHUNK = 128
GDN_HEADS = 8
GDN_HEAD_DIM = 128
GDN_WIDTH = GDN_HEADS * GDN_HEAD_DIM
GDN_HEAD_GROUP = 8
SSM_HEADS = 16
SSM_HEAD_DIM = 64
SSM_WIDTH = SSM_HEADS * SSM_HEAD_DIM
SSM_GROUPS = 2
SSM_STATE = 128
SSM_GROUP_WIDTH = SSM_WIDTH // SSM_GROUPS
VMEM_LIMIT = 60 * 1024 * 1024


def _cparams(sem):
    return pltpu.CompilerParams(dimension_semantics=sem, vmem_limit_bytes=VMEM_LIMIT)


def _dot(a, b):
    return jnp.dot(a, b, preferred_element_type=F32)


def _dot_nt(a, b):
    return lax.dot_general(a, b, (((1,), (1,)), ((), ())), preferred_element_type=F32)


def _bdot(a, b):
    return _dot(a.astype(BF16), b.astype(BF16))


def _split3(a):
    a1 = a.astype(BF16)
    r1 = a - a1.astype(F32)
    a2 = r1.astype(BF16)
    a3 = (r1 - a2.astype(F32)).astype(BF16)
    return a1, a2, a3


def _dot_sel_lhs(sel, b):
    b1, b2, b3 = _split3(b)
    return _dot(sel, b1) + (_dot(sel, b2) + _dot(sel, b3))


def _dot_sel_rhs(a, sel):
    a1, a2, a3 = _split3(a)
    return _dot(a1, sel) + (_dot(a2, sel) + _dot(a3, sel))


def _sigmoid(x):
    return 1.0 / (1.0 + jnp.exp(-x))


def _silu(x):
    return x * _sigmoid(x)


def _softplus(x):
    return jnp.maximum(x, 0.0) + jnp.log1p(jnp.exp(-jnp.abs(x)))


def _rms(x):
    return x * lax.rsqrt(jnp.mean(x * x, axis=-1, keepdims=True) + EPS)


def _ffn_kernel(*refs, cast_once, slabs, emit_weights, resume):
    it = iter(refs)
    x_ref, nw_ref, wg_ref, wu_ref, wd_ref, nnw_ref = (next(it) for _ in range(6))
    if resume:
        next(it), next(it)
    cast_in = [next(it) for _ in cast_once]
    o_ref, on_ref = next(it), next(it)
    wb_refs = [next(it) for _ in range(3)] if emit_weights else ()
    cast_out = [next(it) for _ in cast_once]
    xn_ref = next(it)
    j = pl.program_id(1)
    last = pl.num_programs(1) - 1
    tm = x_ref.shape[0]
    slab_rows = [pl.ds(r * (tm // slabs), tm // slabs) for r in range(slabs)]

    def active(cond):
        return cond & (pl.program_id(0) >= 1) if resume else cond

    def weights():
        if not emit_weights:
            return wg_ref[...], wu_ref[...], wd_ref[...]
        ws = [r[...].astype(BF16) for r in (wg_ref, wu_ref, wd_ref)]
        for dst, w in zip(wb_refs, ws):
            dst[...] = w
        return ws

    for src, dst, once in zip(cast_in, cast_out, cast_once):
        if once:
            @pl.when(j == 0)
            def _(src=src, dst=dst):
                dst[...] = src[...].astype(BF16)
        else:
            dst[...] = src[...].astype(BF16)

    def step(rows, first, final, w):
        wg, wu, wd = w
        if first:
            base = x_ref[rows, :]
            xn = (_rms(base) * nw_ref[...]).astype(BF16)
            xn_ref[rows, :] = xn
        else:
            base = o_ref[rows, :]
            xn = xn_ref[rows, :]
        g = _dot(xn, wg)
        u = _dot(xn, wu)
        h = (0.5 * _silu(g) * u).astype(BF16)
        out = base + _dot(h, wd)
        o_ref[rows, :] = out
        if final:
            on_ref[rows, :] = (_rms(out) * nnw_ref[...]).astype(BF16)

    @pl.when(active(j == 0))
    def _():
        w = weights()
        for rows in slab_rows:
            step(rows, True, False, w)

    @pl.when(active((j > 0) & (j < last)))
    def _():
        step(slice(None), False, False, weights())

    @pl.when(active(j == last))
    def _():
        w = weights()
        for rows in slab_rows:
            step(rows, False, True, w)


def _ffn(x, norm_w, wg, wu, wd, next_norm_w, cast_jobs=(), *, tm, tf, slabs=2, x_single_buffer=False,
         first_tile_only=False, resume_from=None):
    n, d = x.shape
    f = wg.shape[1]
    grid = (1 if first_tile_only else n // tm, f // tf)
    assert grid[1] >= 2 and not (first_tile_only and (cast_jobs or resume_from))
    resume = resume_from is not None
    cast_specs = [pl.BlockSpec(blk, imap) for _, blk, imap, _ in cast_jobs]
    vec_spec = pl.BlockSpec((1, d), lambda i, j: (0, 0))
    if resume:
        row_spec = pl.BlockSpec((tm, d), lambda i, j: (jnp.maximum(i, 1), 0))
        w_col = lambda i, j: (0, jnp.where(i == 0, 0, j))
        w_row = lambda i, j: (jnp.where(i == 0, 0, j), 0)
    else:
        row_spec = pl.BlockSpec((tm, d), lambda i, j: (i, 0))
        w_col = lambda i, j: (0, j)
        w_row = lambda i, j: (j, 0)
    x_spec = pl.BlockSpec(row_spec.block_shape, row_spec.index_map, pipeline_mode=pl.Buffered(1)) \
        if x_single_buffer else row_spec
    w_specs = [pl.BlockSpec((d, tf), w_col), pl.BlockSpec((d, tf), w_col), pl.BlockSpec((tf, d), w_row)]
    any_spec = pl.BlockSpec(memory_space=pl.ANY)
    outs = pl.pallas_call(
        functools.partial(_ffn_kernel, cast_once=tuple(once for _, _, _, once in cast_jobs), slabs=slabs,
                          emit_weights=first_tile_only, resume=resume),
        grid=grid,
        in_specs=[x_spec, vec_spec] + w_specs + [vec_spec] + ([any_spec, any_spec] if resume else []) + cast_specs,
        out_specs=[row_spec, row_spec] + (w_specs if first_tile_only else []) + cast_specs,
        out_shape=[jax.ShapeDtypeStruct((n, d), F32), jax.ShapeDtypeStruct((n, d), BF16)]
        + ([jax.ShapeDtypeStruct(w.shape, BF16) for w in (wg, wu, wd)] if first_tile_only else [])
        + [jax.ShapeDtypeStruct(job[0].shape, BF16) for job in cast_jobs],
        scratch_shapes=[pltpu.VMEM((tm, d), BF16)],
        input_output_aliases={6: 0, 7: 1} if resume else {},
        compiler_params=_cparams(("parallel", "arbitrary")),
        name="ffn",
    )(x, norm_w.reshape(1, d), wg, wu, wd, next_norm_w.reshape(1, d), *(resume_from or ()),
      *[job[0] for job in cast_jobs])
    return outs


def _cast_jobs(grid, mats):
    gi, gj = grid
    jobs = []
    for a in mats:
        r, c = a.shape
        if c % LANE != 0:
            assert r % (gi * 8) == 0
            jobs.append((a, (r // gi, c), lambda i, j: (i, 0), True))
        elif r % (gi * 8) == 0 and c % (gj * LANE) == 0:
            jobs.append((a, (r // gi, c // gj), lambda i, j: (i, j), False))
        elif r % (gj * 8) == 0 and c % (gi * LANE) == 0:
            jobs.append((a, (r // gj, c // gi), lambda i, j: (j, i), False))
        else:
            nb = max(k for k in range(1, gj + 1) if c % (k * LANE) == 0)
            assert r % (gi * 8) == 0
            jobs.append((a, (r // gi, c // nb), lambda i, j, nb=nb: (i, jnp.minimum(j, nb - 1)), False))
    return jobs


def _matmul_nt_kernel(x_ref, w_ref, o_ref):
    o_ref[...] = _dot_nt(x_ref[...], w_ref[...])


def _matmul_nt(xn, wt, *, cols=None, tm=1024, tn):
    n, d = xn.shape
    cols = wt.shape[0] if cols is None else cols
    assert cols % tn == 0 and cols <= wt.shape[0]
    return pl.pallas_call(
        _matmul_nt_kernel,
        grid=(cols // tn, n // tm),
        in_specs=[
            pl.BlockSpec((tm, d), lambda j, i: (i, 0)),
            pl.BlockSpec((tn, d), lambda j, i: (j, 0)),
        ],
        out_specs=pl.BlockSpec((tm, tn), lambda j, i: (i, j)),
        out_shape=jax.ShapeDtypeStruct((n, cols), F32),
        compiler_params=_cparams(("parallel", "parallel")),
        name="in_proj",
    )(xn, wt)


def _causal_conv(halo, cur, w_ref, w_lo):
    c, width = cur.shape
    xx = jnp.concatenate([halo, cur], axis=0)
    y = w_ref[CONV_K - 1:CONV_K, w_lo:w_lo + width] * cur
    for j in range(CONV_K - 1):
        shifted = pltpu.roll(xx, CONV_K - 1 - j, axis=0)
        y = y + w_ref[j:j + 1, w_lo:w_lo + width] * shifted[CONV_HALO:CONV_HALO + c]
    return y


def _halo_rows(cur_ref, prev_ref, r0, cols, use_prev):
    if r0 == 0:
        return prev_ref[:, cols] * use_prev
    return cur_ref[r0 - CONV_HALO:r0, cols]


def _inv_unit_lower_many(mats, row, col):
    c = mats[0].shape[0]
    base = 16
    diag_blk = (row // base) == (col // base)
    eye = jnp.where(row == col, 1.0, 0.0)
    ps = [jnp.where(diag_blk, -a, 0.0) for a in mats]
    ts = [eye + p for p in ps]
    for _ in range(3):
        ps = [_bdot(p, p) for p in ps]
        ts = [t + _bdot(t, p) for t, p in zip(ts, ps)]
    b = 2 * base
    while b <= c:
        off = ((row // b) == (col // b)) & ((row // (b // 2)) != (col // (b // 2)))
        ets = [_bdot(jnp.where(off, a, 0.0), t) for a, t in zip(mats, ts)]
        ts = [t - _bdot(t, et) for t, et in zip(ts, ets)]
        b *= 2
    return ts


def _gdn_kernel(cur_ref, prev_ref, cw_ref, alog_ref, dtb_ref, onw_ref, o_ref, s_ref):
    t_idx = pl.program_id(1)

    @pl.when(t_idx == 0)
    def _():
        s_ref[...] = jnp.zeros_like(s_ref)

    c = MIX_CHUNK
    hd = GDN_HEAD_DIM
    use_prev = jnp.where(t_idx > 0, 1.0, 0.0)
    row = lax.broadcasted_iota(jnp.int32, (c, c), 0)
    col = lax.broadcasted_iota(jnp.int32, (c, c), 1)
    for r0 in range(0, cur_ref.shape[0], c):
        _gdn_chunk(r0, cur_ref, prev_ref, cw_ref, alog_ref, dtb_ref, onw_ref, o_ref, s_ref, use_prev, row, col)


def _gdn_chunk(r0, cur_ref, prev_ref, cw_ref, alog_ref, dtb_ref, onw_ref, o_ref, s_ref, use_prev, row, col):
    c = MIX_CHUNK
    hd = GDN_HEAD_DIM
    rows = slice(r0, r0 + c)
    causal = row >= col
    strict = row > col
    tril = jnp.where(causal, 1.0, 0.0).astype(BF16)

    small = cur_ref[rows, 4 * GDN_WIDTH:4 * GDN_WIDTH + LANE]
    g = -jnp.exp(alog_ref[...]) * _softplus(small + dtb_ref[...])
    beta = _sigmoid(small)
    gc = _dot_sel_lhs(tril, g)
    gc_t = gc.T
    g_last = gc[c - 1:c, :]
    e_gc = jnp.exp(gc)
    e_rest = jnp.exp(g_last - gc)
    g_tot = jnp.exp(g_last)

    def conv(lo):
        cols = slice(lo, lo + hd)
        halo = _halo_rows(cur_ref, prev_ref, r0, cols, use_prev)
        return _silu(_causal_conv(halo, cur_ref[rows, cols], cw_ref, lo))

    def l2norm(x):
        return x * lax.rsqrt(jnp.sum(x * x, axis=-1, keepdims=True) + EPS)

    for g0 in range(0, GDN_HEADS, GDN_HEAD_GROUP):
        heads = range(g0, g0 + GDN_HEAD_GROUP)
        q = {h: l2norm(conv(h * hd)) * (hd ** -0.5) for h in heads}
        k = {h: l2norm(conv(GDN_WIDTH + h * hd)) for h in heads}
        v = {h: conv(2 * GDN_WIDTH + h * hd) for h in heads}
        beta_c = {h: beta[:, GDN_HEADS + h:GDN_HEADS + h + 1] for h in heads}
        decay = {h: jnp.exp(jnp.where(causal, gc[:, h:h + 1] - gc_t[h:h + 1, :], -jnp.inf)) for h in heads}
        kb = {h: k[h] * beta_c[h] for h in heads}
        k_b = {h: k[h].astype(BF16) for h in heads}
        a = {h: jnp.where(strict, _dot_nt(kb[h].astype(BF16), k_b[h]) * decay[h], 0.0) for h in heads}
        qk = {h: (_dot_nt(q[h].astype(BF16), k_b[h]) * decay[h]).astype(BF16) for h in heads}
        t_inv = dict(zip(heads, _inv_unit_lower_many([a[h] for h in heads], row, col)))
        uw = {h: _bdot(t_inv[h], jnp.concatenate([v[h] * beta_c[h], kb[h] * e_gc[:, h:h + 1]], axis=1))
              for h in heads}
        s = {h: s_ref[h] for h in heads}
        wq = {h: jnp.concatenate([uw[h][:, hd:], q[h] * e_gc[:, h:h + 1]], axis=0) for h in heads}
        wq_s = {h: _bdot(wq[h], s[h]) for h in heads}
        v_new = {h: (uw[h][:, :hd] - wq_s[h][:c]).astype(BF16) for h in heads}
        o = {h: wq_s[h][c:] + _dot(qk[h], v_new[h]) for h in heads}
        k_dec_t = {h: (k[h] * e_rest[:, h:h + 1]).T.astype(BF16) for h in heads}
        for h in heads:
            s_ref[h] = s[h] * g_tot[:, h:h + 1] + _dot(k_dec_t[h], v_new[h])
        for h in heads:
            gate = cur_ref[rows, 3 * GDN_WIDTH + h * hd:3 * GDN_WIDTH + (h + 1) * hd]
            o_ref[rows, h * hd:(h + 1) * hd] = (_rms(o[h]) * onw_ref[...] * _silu(gate)).astype(o_ref.dtype)


def _gdn_mixer(proj, conv_w, a_log_row, dt_bias_row, out_norm_w, *, chunks_per_block=2):
    b, t, width = proj.shape
    c = chunks_per_block * MIX_CHUNK
    qkv = 3 * GDN_WIDTH
    halo_blocks = c // CONV_HALO
    return pl.pallas_call(
        _gdn_kernel,
        grid=(b, t // c),
        in_specs=[
            pl.BlockSpec((None, c, width), lambda i, j: (i, j, 0)),
            pl.BlockSpec((None, CONV_HALO, qkv), lambda i, j: (i, jnp.maximum(j * halo_blocks - 1, 0), 0)),
            pl.BlockSpec((CONV_K, qkv), lambda i, j: (0, 0)),
            pl.BlockSpec((1, LANE), lambda i, j: (0, 0)),
            pl.BlockSpec((1, LANE), lambda i, j: (0, 0)),
            pl.BlockSpec((1, GDN_HEAD_DIM), lambda i, j: (0, 0)),
        ],
        out_specs=pl.BlockSpec((None, c, GDN_WIDTH), lambda i, j: (i, j, 0)),
        out_shape=jax.ShapeDtypeStruct((b, t, GDN_WIDTH), BF16),
        scratch_shapes=[pltpu.VMEM((GDN_HEADS, GDN_HEAD_DIM, GDN_HEAD_DIM), F32)],
        compiler_params=_cparams(("parallel", "arbitrary")),
        name="gdn_mixer",
    )(proj, proj, conv_w, a_log_row, dt_bias_row, out_norm_w.reshape(1, GDN_HEAD_DIM))


def _ssd_kernel(cur_ref, prev_ref, cw_ref, cb_ref, alog_ref, dtb_ref, dskip_ref, onw_ref, expand_ref,
                o_ref, st_ref):
    t_idx = pl.program_id(1)

    @pl.when(t_idx == 0)
    def _():
        st_ref[...] = jnp.zeros_like(st_ref)

    c = MIX_CHUNK
    hp = SSM_HEAD_DIM
    gw = SSM_GROUP_WIDTH
    z_lo = 0
    x_lo = SSM_WIDTH
    small_lo = x_lo + SSM_WIDTH + 2 * SSM_GROUPS * SSM_STATE
    use_prev = jnp.where(t_idx > 0, 1.0, 0.0)
    row = lax.broadcasted_iota(jnp.int32, (c, c), 0)
    col = lax.broadcasted_iota(jnp.int32, (c, c), 1)
    causal = row >= col
    tril = jnp.where(causal, 1.0, 0.0).astype(BF16)
    lane = lax.broadcasted_iota(jnp.int32, (c, LANE), 1)
    first_head = lane < hp
    expand = expand_ref[...]

    for r0 in range(0, cur_ref.shape[0], c):
        rows = slice(r0, r0 + c)
        small = cur_ref[rows, small_lo:small_lo + LANE]
        dt = _softplus(small + dtb_ref[...])
        acs = _dot_sel_lhs(tril, dt * -jnp.exp(alog_ref[...]))
        acs_t = acs.T
        dt_w = _dot_sel_rhs(dt, expand)
        acs_w = _dot_sel_rhs(acs, expand)
        e_acs_w = jnp.exp(acs_w)
        e_rest_w = jnp.exp(acs_w[c - 1:c, :] - acs_w)

        def conv(lo, width, r0=r0, rows=rows):
            cols = slice(x_lo + lo, x_lo + lo + width)
            halo = _halo_rows(cur_ref, prev_ref, r0, cols, use_prev)
            return _silu(_causal_conv(halo, cur_ref[rows, cols], cw_ref, lo) + cb_ref[:, lo:lo + width])

        for g in range(SSM_GROUPS):
            bg = conv(SSM_WIDTH + g * SSM_STATE, SSM_STATE)
            cg = conv(SSM_WIDTH + (SSM_GROUPS + g) * SSM_STATE, SSM_STATE)
            bg_b = bg.astype(BF16)
            cg_b = cg.astype(BF16)
            cb = _dot_nt(cg_b, bg_b)
            cols = slice(g * gw, (g + 1) * gw)
            xs = conv(g * gw, gw)
            xc = xs * dt_w[:, cols]
            state = st_ref[g]
            y = _dot(cg_b, state.astype(BF16)) * e_acs_w[:, cols]
            chunk_state = _dot(bg.T.astype(BF16), (xc * e_rest_w[:, cols]).astype(BF16))
            st_ref[g] = state * e_acs_w[c - 1:c, cols] + chunk_state
            pieces = []
            for pr in range(gw // LANE):
                xp = xc[:, pr * LANE:(pr + 1) * LANE]
                lhs = []
                for r in range(2):
                    hidx = g * (SSM_HEADS // SSM_GROUPS) + 2 * pr + r
                    seg = jnp.exp(jnp.where(causal, acs[:, hidx:hidx + 1] - acs_t[hidx:hidx + 1, :], -jnp.inf))
                    lhs.append((cb * seg).astype(BF16))
                rhs = jnp.concatenate([jnp.where(first_head, xp, 0.0), jnp.where(first_head, 0.0, xp)], axis=0)
                pieces.append(_dot(jnp.concatenate(lhs, axis=1), rhs.astype(BF16)))
            y = y + jnp.concatenate(pieces, axis=1) + xs * dskip_ref[:, cols]
            y = y * _silu(cur_ref[rows, z_lo + g * gw:z_lo + (g + 1) * gw])
            o_ref[rows, cols] = (_rms(y) * onw_ref[:, cols]).astype(o_ref.dtype)


def _ssd_mixer(proj, conv_w, conv_b, a_log_row, dt_bias_row, d_skip_w, out_norm_w, expand, *, chunks_per_block=2):
    b, t, width = proj.shape
    c = chunks_per_block * MIX_CHUNK
    xbc = SSM_WIDTH + 2 * SSM_GROUPS * SSM_STATE
    halo_blocks = c // CONV_HALO
    const = lambda i, j: (0, 0)
    return pl.pallas_call(
        _ssd_kernel,
        grid=(b, t // c),
        in_specs=[
            pl.BlockSpec((None, c, width), lambda i, j: (i, j, 0)),
            pl.BlockSpec((None, CONV_HALO, width), lambda i, j: (i, jnp.maximum(j * halo_blocks - 1, 0), 0)),
            pl.BlockSpec((CONV_K, xbc), const),
            pl.BlockSpec((1, xbc), const),
            pl.BlockSpec((1, LANE), const),
            pl.BlockSpec((1, LANE), const),
            pl.BlockSpec((1, SSM_WIDTH), const),
            pl.BlockSpec((1, SSM_WIDTH), const),
            pl.BlockSpec((LANE, SSM_WIDTH), const),
        ],
        out_specs=pl.BlockSpec((None, c, SSM_WIDTH), lambda i, j: (i, j, 0)),
        out_shape=jax.ShapeDtypeStruct((b, t, SSM_WIDTH), BF16),
        scratch_shapes=[pltpu.VMEM((SSM_GROUPS, SSM_STATE, SSM_GROUP_WIDTH), F32)],
        compiler_params=_cparams(("parallel", "arbitrary")),
        name="ssd_mixer",
    )(proj, proj, conv_w, conv_b.reshape(1, xbc), a_log_row, dt_bias_row, d_skip_w, out_norm_w.reshape(1, SSM_WIDTH),
      expand)


def _out_proj_kernel(h_ref, og_ref, os_ref, wg_ref, ws_ref, o_ref):
    o_ref[...] = h_ref[...] + _dot(og_ref[...], wg_ref[...]) + _dot(os_ref[...], ws_ref[...])


def _out_proj(h, o_gdn, o_ssm, w, *, tm=512):
    n, d = h.shape
    assert GDN_WIDTH == SSM_WIDTH
    return pl.pallas_call(
        _out_proj_kernel,
        grid=(n // tm,),
        in_specs=[
            pl.BlockSpec((tm, d), lambda i: (i, 0)),
            pl.BlockSpec((tm, GDN_WIDTH), lambda i: (i, 0)),
            pl.BlockSpec((tm, SSM_WIDTH), lambda i: (i, 0)),
            pl.BlockSpec((GDN_WIDTH, d), lambda i: (0, 0)),
            pl.BlockSpec((SSM_WIDTH, d), lambda i: (1, 0)),
        ],
        out_specs=pl.BlockSpec((tm, d), lambda i: (i, 0)),
        out_shape=jax.ShapeDtypeStruct((n, d), F32),
        compiler_params=_cparams(("parallel",)),
        name="out_proj",
    )(h, o_gdn, o_ssm, w, w)


def _ple_kernel(h_ref, hn_ref, p_ref, wg_ref, wp_ref, pnw_ref, fnw_ref, o_ref, *, slabs):
    for lo, hi in zip(slabs[:-1], slabs[1:]):
        rows = pl.ds(lo, hi - lo)
        gate = _sigmoid(_dot(hn_ref[rows, :], wg_ref[...]))
        emb = _rms(_dot(p_ref[rows, :].astype(BF16), wp_ref[...])) * pnw_ref[...]
        o_ref[rows, :] = _rms(h_ref[rows, :] + gate * emb) * fnw_ref[...]


def _ple(h, hn, p, w_gate, w_proj, post_norm_w, final_norm_w, *, tm=512, slabs=(0, 256, 512)):
    n, d = h.shape
    pd = p.shape[1]
    const = lambda i: (0, 0)
    return pl.pallas_call(
        functools.partial(_ple_kernel, slabs=slabs),
        grid=(n // tm,),
        in_specs=[
            pl.BlockSpec((tm, d), lambda i: (i, 0)),
            pl.BlockSpec((tm, d), lambda i: (i, 0)),
            pl.BlockSpec((tm, pd), lambda i: (i, 0)),
            pl.BlockSpec((d, d), const),
            pl.BlockSpec((pd, d), const),
            pl.BlockSpec((1, d), const),
            pl.BlockSpec((1, d), const),
        ],
        out_specs=pl.BlockSpec((tm, d), lambda i: (i, 0)),
        out_shape=jax.ShapeDtypeStruct((n, d), F32),
        compiler_params=_cparams(("parallel",)),
        name="ple_final",
    )(h, hn, p, w_gate, w_proj, post_norm_w.reshape(1, d), final_norm_w.reshape(1, d))


def _lane_row(v):
    return jnp.zeros((1, LANE), F32).at[0, :v.shape[0]].set(v.astype(F32))


def kernel(x, p, ffn1_norm, ffn1_w_gate, ffn1_w_up, ffn1_w_down, mix_norm, w_in, gdn_conv_w, gdn_a_log, gdn_dt_bias, gdn_out_norm, ssm_conv_w, ssm_conv_b, ssm_a_log, ssm_dt_bias, ssm_d, ssm_out_norm, w_out, ffn2_norm, ffn2_w_gate, ffn2_w_up, ffn2_w_down, ple_norm, ple_w_gate, ple_w_proj, ple_post_norm, final_norm):
    bsz, seq, d = x.shape
    n = bsz * seq
    depth = p.shape[0]
    o_z = 4 * GDN_WIDTH + 2 * GDN_HEADS
    gdn_cols = 4 * GDN_WIDTH + LANE
    expand = (jnp.arange(LANE)[:, None] == (jnp.arange(SSM_WIDTH)[None, :] // SSM_HEAD_DIM)).astype(BF16)
    ffn_tm, ffn1_tf, ffn2_tf = 1024, 256, 512

    h = x.reshape(n, d)
    for i in range(depth):
        h1, hn1, wg1, wu1, wd1 = _ffn(
            h, ffn1_norm[i], ffn1_w_gate[i], ffn1_w_up[i], ffn1_w_down[i], mix_norm[i],
            tm=ffn_tm, tf=ffn1_tf, x_single_buffer=True, first_tile_only=True)
        wi_t = jnp.swapaxes(w_in[i], 0, 1)
        jobs = _cast_jobs((n // ffn_tm, ffn1_w_gate.shape[2] // ffn1_tf),
                          [ffn2_w_gate[i], ffn2_w_up[i], ffn2_w_down[i], w_out[i], ple_w_gate[i], wi_t])
        h, hn, wg2, wu2, wd2, wo, wpg, wi_t = _ffn(
            h, ffn1_norm[i], wg1, wu1, wd1, mix_norm[i], jobs, tm=ffn_tm, tf=ffn1_tf, resume_from=(h1, hn1))
        pad = jnp.zeros((LANE - SSM_HEADS, d), BF16)
        w_ssd_t = jnp.concatenate([wi_t[o_z:], pad], axis=0)
        proj_gdn = _matmul_nt(hn, wi_t, cols=gdn_cols, tn=gdn_cols // 3)
        proj_ssd = _matmul_nt(hn, w_ssd_t, tn=w_ssd_t.shape[0] // 3)
        o_gdn = _gdn_mixer(proj_gdn.reshape(bsz, seq, -1), gdn_conv_w[i], _lane_row(gdn_a_log[i]),
                           _lane_row(gdn_dt_bias[i]), gdn_out_norm[i])
        o_ssm = _ssd_mixer(proj_ssd.reshape(bsz, seq, -1), ssm_conv_w[i], ssm_conv_b[i],
                           _lane_row(ssm_a_log[i]), _lane_row(ssm_dt_bias[i]),
                           jnp.repeat(ssm_d[i].astype(F32), SSM_HEAD_DIM).reshape(1, SSM_WIDTH),
                           ssm_out_norm[i], expand)
        h = _out_proj(h, o_gdn.reshape(n, GDN_WIDTH), o_ssm.reshape(n, SSM_WIDTH), wo)
        h, hn = _ffn(h, ffn2_norm[i], wg2, wu2, wd2, ple_norm[i], tm=ffn_tm, tf=ffn2_tf, x_single_buffer=True)
        assert i == depth - 1, "depth > 1 would need an un-normalised variant of the embedding kernel"
        h = _ple(h, hn, p[i].reshape(n, -1), wpg, ple_w_proj[i].astype(BF16), ple_post_norm[i], final_norm)
    return h.reshape(bsz, seq, d)
```

```python
import functools

import jax
import jax.numpy as jnp
from jax import lax
from jax.experimental import pallas as pl
from jax.experimental.pallas import tpu as pltpu

F32 = jnp.float32
BF16 = jnp.bfloat16
EPS = 1e-6

LANE = 128
CONV_K = 4
CONV_HALO = 8
MIX_CHUNK = 128
GDN_HEADS = 8
GDN_HEAD_DIM = 128
GDN_WIDTH = GDN_HEADS * GDN_HEAD_DIM
GDN_HEAD_GROUP = 8
SSM_HEADS = 16
SSM_HEAD_DIM = 64
SSM_WIDTH = SSM_HEADS * SSM_HEAD_DIM
SSM_GROUPS = 2
SSM_STATE = 128
SSM_GROUP_WIDTH = SSM_WIDTH // SSM_GROUPS
VMEM_LIMIT = 60 * 1024 * 1024


def _cparams(sem):
    return pltpu.CompilerParams(dimension_semantics=sem, vmem_limit_bytes=VMEM_LIMIT)


def _dot(a, b):
    return jnp.dot(a, b, preferred_element_type=F32)


def _dot_nt(a, b):
    return lax.dot_general(a, b, (((1,), (1,)), ((), ())), preferred_element_type=F32)


def _bdot(a, b):
    return _dot(a.astype(BF16), b.astype(BF16))


def _split3(a):
    a1 = a.astype(BF16)
    r1 = a - a1.astype(F32)
    a2 = r1.astype(BF16)
    a3 = (r1 - a2.astype(F32)).astype(BF16)
    return a1, a2, a3


def _dot_sel_lhs(sel, b):
    b1, b2, b3 = _split3(b)
    return _dot(sel, b1) + (_dot(sel, b2) + _dot(sel, b3))


def _dot_sel_rhs(a, sel):
    a1, a2, a3 = _split3(a)
    return _dot(a1, sel) + (_dot(a2, sel) + _dot(a3, sel))


def _sigmoid(x):
    return 1.0 / (1.0 + jnp.exp(-x))


def _silu(x):
    return x * _sigmoid(x)


def _softplus(x):
    return jnp.maximum(x, 0.0) + jnp.log1p(jnp.exp(-jnp.abs(x)))


def _rms(x):
    return x * lax.rsqrt(jnp.mean(x * x, axis=-1, keepdims=True) + EPS)


def _ffn_kernel(*refs, cast_once, slabs, emit_weights, resume):
    it = iter(refs)
    x_ref, nw_ref, wg_ref, wu_ref, wd_ref, nnw_ref = (next(it) for _ in range(6))
    if resume:
        next(it), next(it)
    cast_in = [next(it) for _ in cast_once]
    o_ref, on_ref = next(it), next(it)
    wb_refs = [next(it) for _ in range(3)] if emit_weights else ()
    cast_out = [next(it) for _ in cast_once]
    xn_ref = next(it)
    j = pl.program_id(1)
    last = pl.num_programs(1) - 1
    tm = x_ref.shape[0]
    slab_rows = [pl.ds(r * (tm // slabs), tm // slabs) for r in range(slabs)]

    def active(cond):
        return cond & (pl.program_id(0) >= 1) if resume else cond

    def weights():
        if not emit_weights:
            return wg_ref[...], wu_ref[...], wd_ref[...]
        ws = [r[...].astype(BF16) for r in (wg_ref, wu_ref, wd_ref)]
        for dst, w in zip(wb_refs, ws):
            dst[...] = w
        return ws

    for src, dst, once in zip(cast_in, cast_out, cast_once):
        if once:
            @pl.when(j == 0)
            def _(src=src, dst=dst):
                dst[...] = src[...].astype(BF16)
        else:
            dst[...] = src[...].astype(BF16)

    def step(rows, first, final, w):
        wg, wu, wd = w
        if first:
            base = x_ref[rows, :]
            xn = (_rms(base) * nw_ref[...]).astype(BF16)
            xn_ref[rows, :] = xn
        else:
            base = o_ref[rows, :]
            xn = xn_ref[rows, :]
        g = _dot(xn, wg)
        u = _dot(xn, wu)
        h = (0.5 * _silu(g) * u).astype(BF16)
        out = base + _dot(h, wd)
        o_ref[rows, :] = out
        if final:
            on_ref[rows, :] = (_rms(out) * nnw_ref[...]).astype(BF16)

    @pl.when(active(j == 0))
    def _():
        w = weights()
        for rows in slab_rows:
            step(rows, True, False, w)

    @pl.when(active((j > 0) & (j < last)))
    def _():
        step(slice(None), False, False, weights())

    @pl.when(active(j == last))
    def _():
        w = weights()
        for rows in slab_rows:
            step(rows, False, True, w)


def _ffn(x, norm_w, wg, wu, wd, next_norm_w, cast_jobs=(), *, tm, tf, slabs=2, x_single_buffer=False,
         first_tile_only=False, resume_from=None):
    n, d = x.shape
    f = wg.shape[1]
    grid = (1 if first_tile_only else n // tm, f // tf)
    assert grid[1] >= 2 and not (first_tile_only and (cast_jobs or resume_from))
    resume = resume_from is not None
    cast_specs = [pl.BlockSpec(blk, imap) for _, blk, imap, _ in cast_jobs]
    vec_spec = pl.BlockSpec((1, d), lambda i, j: (0, 0))
    if resume:
        row_spec = pl.BlockSpec((tm, d), lambda i, j: (jnp.maximum(i, 1), 0))
        w_col = lambda i, j: (0, jnp.where(i == 0, 0, j))
        w_row = lambda i, j: (jnp.where(i == 0, 0, j), 0)
    else:
        row_spec = pl.BlockSpec((tm, d), lambda i, j: (i, 0))
        w_col = lambda i, j: (0, j)
        w_row = lambda i, j: (j, 0)
    x_spec = pl.BlockSpec(row_spec.block_shape, row_spec.index_map, pipeline_mode=pl.Buffered(1)) \
        if x_single_buffer else row_spec
    w_specs = [pl.BlockSpec((d, tf), w_col), pl.BlockSpec((d, tf), w_col), pl.BlockSpec((tf, d), w_row)]
    any_spec = pl.BlockSpec(memory_space=pl.ANY)
    outs = pl.pallas_call(
        functools.partial(_ffn_kernel, cast_once=tuple(once for _, _, _, once in cast_jobs), slabs=slabs,
                          emit_weights=first_tile_only, resume=resume),
        grid=grid,
        in_specs=[x_spec, vec_spec] + w_specs + [vec_spec] + ([any_spec, any_spec] if resume else []) + cast_specs,
        out_specs=[row_spec, row_spec] + (w_specs if first_tile_only else []) + cast_specs,
        out_shape=[jax.ShapeDtypeStruct((n, d), F32), jax.ShapeDtypeStruct((n, d), BF16)]
        + ([jax.ShapeDtypeStruct(w.shape, BF16) for w in (wg, wu, wd)] if first_tile_only else [])
        + [jax.ShapeDtypeStruct(job[0].shape, BF16) for job in cast_jobs],
        scratch_shapes=[pltpu.VMEM((tm, d), BF16)],
        input_output_aliases={6: 0, 7: 1} if resume else {},
        compiler_params=_cparams(("parallel", "arbitrary")),
        name="ffn",
    )(x, norm_w.reshape(1, d), wg, wu, wd, next_norm_w.reshape(1, d), *(resume_from or ()),
      *[job[0] for job in cast_jobs])
    return outs


def _cast_jobs(grid, mats):
    gi, gj = grid
    jobs = []
    for a in mats:
        r, c = a.shape
        if c % LANE != 0:
            assert r % (gi * 8) == 0
            jobs.append((a, (r // gi, c), lambda i, j: (i, 0), True))
        elif r % (gi * 8) == 0 and c % (gj * LANE) == 0:
            jobs.append((a, (r // gi, c // gj), lambda i, j: (i, j), False))
        elif r % (gj * 8) == 0 and c % (gi * LANE) == 0:
            jobs.append((a, (r // gj, c // gi), lambda i, j: (j, i), False))
        else:
            nb = max(k for k in range(1, gj + 1) if c % (k * LANE) == 0)
            assert r % (gi * 8) == 0
            jobs.append((a, (r // gi, c // nb), lambda i, j, nb=nb: (i, jnp.minimum(j, nb - 1)), False))
    return jobs


def _matmul_nt_kernel(x_ref, w_ref, o_ref):
    o_ref[...] = _dot_nt(x_ref[...], w_ref[...])


def _matmul_nt(xn, wt, *, lo, cols, tm=1024, tn):
    n, d = xn.shape
    assert cols % tn == 0 and lo % tn == 0 and lo + cols <= wt.shape[0]
    off = lo // tn
    return pl.pallas_call(
        _matmul_nt_kernel,
        grid=(cols // tn, n // tm),
        in_specs=[
            pl.BlockSpec((tm, d), lambda j, i: (i, 0)),
            pl.BlockSpec((tn, d), lambda j, i: (j + off, 0)),
        ],
        out_specs=pl.BlockSpec((tm, tn), lambda j, i: (i, j)),
        out_shape=jax.ShapeDtypeStruct((n, cols), F32),
        compiler_params=_cparams(("parallel", "parallel")),
        name="in_proj",
    )(xn, wt)


def _proj_conv_kernel(x_ref, w_ref, cw_ref, cb_ref, o_ref, halo_ref, *, tiles_per_seq, slab):
    tm, tn = o_ref.shape

    @pl.when(pl.program_id(1) % tiles_per_seq == 0)
    def _():
        halo_ref[...] = jnp.zeros_like(halo_ref)

    def project(s0):
        return _dot_nt(x_ref[s0:s0 + slab, :], w_ref[...])

    tail = halo_ref[...]
    raw = project(0)
    for s0 in range(0, tm, slab):
        nxt = project(s0 + slab) if s0 + slab < tm else None
        for r0 in range(0, slab, MIX_CHUNK):
            for c0 in range(0, tn, LANE):
                cols = slice(c0, c0 + LANE)
                halo = tail[:, cols] if r0 == 0 else raw[r0 - CONV_HALO:r0, cols]
                y = _causal_conv(halo, raw[r0:r0 + MIX_CHUNK, cols], cw_ref, c0) + cb_ref[:, cols]
                o_ref[s0 + r0:s0 + r0 + MIX_CHUNK, cols] = _silu(y)
        tail = raw[slab - CONV_HALO:, :]
        raw = nxt
    halo_ref[...] = tail


def _proj_conv(xn, wt, conv_w, conv_b, *, cols, seq, tm=2048, tn, slab=512):
    n, d = xn.shape
    assert cols % tn == 0 and seq % tm == 0 and tm % slab == 0 and slab % MIX_CHUNK == 0
    return pl.pallas_call(
        functools.partial(_proj_conv_kernel, tiles_per_seq=seq // tm, slab=slab),
        grid=(cols // tn, n // tm),
        in_specs=[
            pl.BlockSpec((tm, d), lambda j, i: (i, 0)),
            pl.BlockSpec((tn, d), lambda j, i: (j, 0)),
            pl.BlockSpec((CONV_K, tn), lambda j, i: (0, j)),
            pl.BlockSpec((1, tn), lambda j, i: (0, j)),
        ],
        out_specs=pl.BlockSpec((tm, tn), lambda j, i: (i, j)),
        out_shape=jax.ShapeDtypeStruct((n, cols), F32),
        scratch_shapes=[pltpu.VMEM((CONV_HALO, tn), F32)],
        compiler_params=_cparams(("parallel", "arbitrary")),
        name="in_proj_conv",
    )(xn, wt, conv_w, conv_b.reshape(1, cols))


def _causal_conv(halo, cur, w_ref, w_lo):
    c, width = cur.shape
    xx = jnp.concatenate([halo, cur], axis=0)
    y = w_ref[CONV_K - 1:CONV_K, w_lo:w_lo + width] * cur
    for j in range(CONV_K - 1):
        shifted = pltpu.roll(xx, CONV_K - 1 - j, axis=0)
        y = y + w_ref[j:j + 1, w_lo:w_lo + width] * shifted[CONV_HALO:CONV_HALO + c]
    return y


def _inv_unit_lower_many(mats, row, col):
    c = mats[0].shape[0]
    base = 16
    diag_blk = (row // base) == (col // base)
    eye = jnp.where(row == col, 1.0, 0.0)
    ps = [jnp.where(diag_blk, -a, 0.0) for a in mats]
    ts = [eye + p for p in ps]
    for _ in range(3):
        ps = [_bdot(p, p) for p in ps]
        ts = [t + _bdot(t, p) for t, p in zip(ts, ps)]
    b = 2 * base
    while b <= c:
        off = ((row // b) == (col // b)) & ((row // (b // 2)) != (col // (b // 2)))
        ets = [_bdot(jnp.where(off, a, 0.0), t) for a, t in zip(mats, ts)]
        ts = [t - _bdot(t, et) for t, et in zip(ts, ets)]
        b *= 2
    return ts


def _gdn_kernel(act_ref, raw_ref, alog_ref, dtb_ref, onw_ref, o_ref, s_ref):
    @pl.when(pl.program_id(1) == 0)
    def _():
        s_ref[...] = jnp.zeros_like(s_ref)

    c = MIX_CHUNK
    row = lax.broadcasted_iota(jnp.int32, (c, c), 0)
    col = lax.broadcasted_iota(jnp.int32, (c, c), 1)
    for r0 in range(0, act_ref.shape[0], c):
        _gdn_chunk(r0, act_ref, raw_ref, alog_ref, dtb_ref, onw_ref, o_ref, s_ref, row, col)


def _gdn_chunk(r0, act_ref, raw_ref, alog_ref, dtb_ref, onw_ref, o_ref, s_ref, row, col):
    c = MIX_CHUNK
    hd = GDN_HEAD_DIM
    rows = slice(r0, r0 + c)
    causal = row >= col
    strict = row > col
    tril = jnp.where(causal, 1.0, 0.0).astype(BF16)

    small = raw_ref[rows, GDN_WIDTH:GDN_WIDTH + LANE]
    g = -jnp.exp(alog_ref[...]) * _softplus(small + dtb_ref[...])
    beta = _sigmoid(small)
    gc = _dot_sel_lhs(tril, g)
    gc_t = gc.T
    g_last = gc[c - 1:c, :]
    e_gc = jnp.exp(gc)
    e_rest = jnp.exp(g_last - gc)
    g_tot = jnp.exp(g_last)

    def conv(lo):
        return act_ref[rows, lo:lo + hd]

    def l2norm(x):
        return x * lax.rsqrt(jnp.sum(x * x, axis=-1, keepdims=True) + EPS)

    for g0 in range(0, GDN_HEADS, GDN_HEAD_GROUP):
        heads = range(g0, g0 + GDN_HEAD_GROUP)
        q = {h: l2norm(conv(h * hd)) * (hd ** -0.5) for h in heads}
        k = {h: l2norm(conv(GDN_WIDTH + h * hd)) for h in heads}
        v = {h: conv(2 * GDN_WIDTH + h * hd) for h in heads}
        beta_c = {h: beta[:, GDN_HEADS + h:GDN_HEADS + h + 1] for h in heads}
        decay = {h: jnp.exp(jnp.where(causal, gc[:, h:h + 1] - gc_t[h:h + 1, :], -jnp.inf)) for h in heads}
        kb = {h: k[h] * beta_c[h] for h in heads}
        k_b = {h: k[h].astype(BF16) for h in heads}
        a = {h: jnp.where(strict, _dot_nt(kb[h].astype(BF16), k_b[h]) * decay[h], 0.0) for h in heads}
        qk = {h: (_dot_nt(q[h].astype(BF16), k_b[h]) * decay[h]).astype(BF16) for h in heads}
        t_inv = dict(zip(heads, _inv_unit_lower_many([a[h] for h in heads], row, col)))
        uw = {h: _bdot(t_inv[h], jnp.concatenate([v[h] * beta_c[h], kb[h] * e_gc[:, h:h + 1]], axis=1))
              for h in heads}
        s = {h: s_ref[h] for h in heads}
        wq = {h: jnp.concatenate([uw[h][:, hd:], q[h] * e_gc[:, h:h + 1]], axis=0) for h in heads}
        wq_s = {h: _bdot(wq[h], s[h]) for h in heads}
        v_new = {h: (uw[h][:, :hd] - wq_s[h][:c]).astype(BF16) for h in heads}
        o = {h: wq_s[h][c:] + _dot(qk[h], v_new[h]) for h in heads}
        k_dec_t = {h: (k[h] * e_rest[:, h:h + 1]).T.astype(BF16) for h in heads}
        for h in heads:
            s_ref[h] = s[h] * g_tot[:, h:h + 1] + _dot(k_dec_t[h], v_new[h])
        for h in heads:
            gate = raw_ref[rows, h * hd:(h + 1) * hd]
            o_ref[rows, h * hd:(h + 1) * hd] = (_rms(o[h]) * onw_ref[...] * _silu(gate)).astype(o_ref.dtype)


def _gdn_mixer(act, raw, raw_block, a_log_row, dt_bias_row, out_norm_w, *, chunks_per_block=2):
    b, t, _ = act.shape
    c = chunks_per_block * MIX_CHUNK
    return pl.pallas_call(
        _gdn_kernel,
        grid=(b, t // c),
        in_specs=[
            pl.BlockSpec((None, c, act.shape[2]), lambda i, j: (i, j, 0)),
            pl.BlockSpec((None, c, GDN_WIDTH + LANE), lambda i, j: (i, j, raw_block)),
            pl.BlockSpec((1, LANE), lambda i, j: (0, 0)),
            pl.BlockSpec((1, LANE), lambda i, j: (0, 0)),
            pl.BlockSpec((1, GDN_HEAD_DIM), lambda i, j: (0, 0)),
        ],
        out_specs=pl.BlockSpec((None, c, GDN_WIDTH), lambda i, j: (i, j, 0)),
        out_shape=jax.ShapeDtypeStruct((b, t, GDN_WIDTH), BF16),
        scratch_shapes=[pltpu.VMEM((GDN_HEADS, GDN_HEAD_DIM, GDN_HEAD_DIM), F32)],
        compiler_params=_cparams(("parallel", "arbitrary")),
        name="gdn_mixer",
    )(act, raw, a_log_row, dt_bias_row, out_norm_w.reshape(1, GDN_HEAD_DIM))


def _ssd_kernel(act_ref, raw_ref, alog_ref, dtb_ref, dskip_ref, onw_ref, expand_ref, o_ref, st_ref):
    @pl.when(pl.program_id(1) == 0)
    def _():
        st_ref[...] = jnp.zeros_like(st_ref)

    c = MIX_CHUNK
    hp = SSM_HEAD_DIM
    gw = SSM_GROUP_WIDTH
    row = lax.broadcasted_iota(jnp.int32, (c, c), 0)
    col = lax.broadcasted_iota(jnp.int32, (c, c), 1)
    causal = row >= col
    tril = jnp.where(causal, 1.0, 0.0).astype(BF16)
    lane = lax.broadcasted_iota(jnp.int32, (c, LANE), 1)
    first_head = lane < hp
    expand = expand_ref[...]

    for r0 in range(0, act_ref.shape[0], c):
        rows = slice(r0, r0 + c)
        small = raw_ref[rows, SSM_WIDTH:SSM_WIDTH + LANE]
        dt = _softplus(small + dtb_ref[...])
        acs = _dot_sel_lhs(tril, dt * -jnp.exp(alog_ref[...]))
        acs_t = acs.T
        dt_w = _dot_sel_rhs(dt, expand)
        acs_w = _dot_sel_rhs(acs, expand)
        e_acs_w = jnp.exp(acs_w)
        e_rest_w = jnp.exp(acs_w[c - 1:c, :] - acs_w)

        def conv(lo, width, rows=rows):
            return act_ref[rows, lo:lo + width]

        for g in range(SSM_GROUPS):
            bg = conv(SSM_WIDTH + g * SSM_STATE, SSM_STATE)
            cg = conv(SSM_WIDTH + (SSM_GROUPS + g) * SSM_STATE, SSM_STATE)
            bg_b = bg.astype(BF16)
            cg_b = cg.astype(BF16)
            cb = _dot_nt(cg_b, bg_b)
            cols = slice(g * gw, (g + 1) * gw)
            xs = conv(g * gw, gw)
            xc = xs * dt_w[:, cols]
            state = st_ref[g]
            y = _dot(cg_b, state.astype(BF16)) * e_acs_w[:, cols]
            chunk_state = _dot(bg.T.astype(BF16), (xc * e_rest_w[:, cols]).astype(BF16))
            st_ref[g] = state * e_acs_w[c - 1:c, cols] + chunk_state
            pieces = []
            for pr in range(gw // LANE):
                xp = xc[:, pr * LANE:(pr + 1) * LANE]
                lhs = []
                for r in range(2):
                    hidx = g * (SSM_HEADS // SSM_GROUPS) + 2 * pr + r
                    seg = jnp.exp(jnp.where(causal, acs[:, hidx:hidx + 1] - acs_t[hidx:hidx + 1, :], -jnp.inf))
                    lhs.append((cb * seg).astype(BF16))
                rhs = jnp.concatenate([jnp.where(first_head, xp, 0.0), jnp.where(first_head, 0.0, xp)], axis=0)
                pieces.append(_dot(jnp.concatenate(lhs, axis=1), rhs.astype(BF16)))
            y = y + jnp.concatenate(pieces, axis=1) + xs * dskip_ref[:, cols]
            y = y * _silu(raw_ref[rows, g * gw:(g + 1) * gw])
            o_ref[rows, cols] = (_rms(y) * onw_ref[:, cols]).astype(o_ref.dtype)


def _ssd_mixer(act, raw, raw_block, a_log_row, dt_bias_row, d_skip_w, out_norm_w, expand, *, chunks_per_block=2):
    b, t, _ = act.shape
    c = chunks_per_block * MIX_CHUNK
    const = lambda i, j: (0, 0)
    return pl.pallas_call(
        _ssd_kernel,
        grid=(b, t // c),
        in_specs=[
            pl.BlockSpec((None, c, act.shape[2]), lambda i, j: (i, j, 0)),
            pl.BlockSpec((None, c, SSM_WIDTH + LANE), lambda i, j: (i, j, raw_block)),
            pl.BlockSpec((1, LANE), const),
            pl.BlockSpec((1, LANE), const),
            pl.BlockSpec((1, SSM_WIDTH), const),
            pl.BlockSpec((1, SSM_WIDTH), const),
            pl.BlockSpec((LANE, SSM_WIDTH), const),
        ],
        out_specs=pl.BlockSpec((None, c, SSM_WIDTH), lambda i, j: (i, j, 0)),
        out_shape=jax.ShapeDtypeStruct((b, t, SSM_WIDTH), BF16),
        scratch_shapes=[pltpu.VMEM((SSM_GROUPS, SSM_STATE, SSM_GROUP_WIDTH), F32)],
        compiler_params=_cparams(("parallel", "arbitrary")),
        name="ssd_mixer",
    )(act, raw, a_log_row, dt_bias_row, d_skip_w, out_norm_w.reshape(1, SSM_WIDTH), expand)


def _out_proj_kernel(h_ref, og_ref, os_ref, wg_ref, ws_ref, o_ref):
    o_ref[...] = h_ref[...] + _dot(og_ref[...], wg_ref[...]) + _dot(os_ref[...], ws_ref[...])


def _out_proj(h, o_gdn, o_ssm, w, *, tm=512):
    n, d = h.shape
    assert GDN_WIDTH == SSM_WIDTH
    return pl.pallas_call(
        _out_proj_kernel,
        grid=(n // tm,),
        in_specs=[
            pl.BlockSpec((tm, d), lambda i: (i, 0)),
            pl.BlockSpec((tm, GDN_WIDTH), lambda i: (i, 0)),
            pl.BlockSpec((tm, SSM_WIDTH), lambda i: (i, 0)),
            pl.BlockSpec((GDN_WIDTH, d), lambda i: (0, 0)),
            pl.BlockSpec((SSM_WIDTH, d), lambda i: (1, 0)),
        ],
        out_specs=pl.BlockSpec((tm, d), lambda i: (i, 0)),
        out_shape=jax.ShapeDtypeStruct((n, d), F32),
        compiler_params=_cparams(("parallel",)),
        name="out_proj",
    )(h, o_gdn, o_ssm, w, w)


def _ple_kernel(h_ref, hn_ref, p_ref, wg_ref, wp_ref, pnw_ref, fnw_ref, o_ref, *, slabs):
    for lo, hi in zip(slabs[:-1], slabs[1:]):
        rows = pl.ds(lo, hi - lo)
        gate = _sigmoid(_dot(hn_ref[rows, :], wg_ref[...]))
        emb = _rms(_dot(p_ref[rows, :].astype(BF16), wp_ref[...])) * pnw_ref[...]
        o_ref[rows, :] = _rms(h_ref[rows, :] + gate * emb) * fnw_ref[...]


def _ple(h, hn, p, w_gate, w_proj, post_norm_w, final_norm_w, *, tm=512, slabs=(0, 256, 512)):
    n, d = h.shape
    pd = p.shape[1]
    const = lambda i: (0, 0)
    return pl.pallas_call(
        functools.partial(_ple_kernel, slabs=slabs),
        grid=(n // tm,),
        in_specs=[
            pl.BlockSpec((tm, d), lambda i: (i, 0)),
            pl.BlockSpec((tm, d), lambda i: (i, 0)),
            pl.BlockSpec((tm, pd), lambda i: (i, 0)),
            pl.BlockSpec((d, d), const),
            pl.BlockSpec((pd, d), const),
            pl.BlockSpec((1, d), const),
            pl.BlockSpec((1, d), const),
        ],
        out_specs=pl.BlockSpec((tm, d), lambda i: (i, 0)),
        out_shape=jax.ShapeDtypeStruct((n, d), F32),
        compiler_params=_cparams(("parallel",)),
        name="ple_final",
    )(h, hn, p, w_gate, w_proj, post_norm_w.reshape(1, d), final_norm_w.reshape(1, d))


def _lane_row(v):
    return jnp.zeros((1, LANE), F32).at[0, :v.shape[0]].set(v.astype(F32))


def kernel(x, p, ffn1_norm, ffn1_w_gate, ffn1_w_up, ffn1_w_down, mix_norm, w_in, gdn_conv_w, gdn_a_log, gdn_dt_bias, gdn_out_norm, ssm_conv_w, ssm_conv_b, ssm_a_log, ssm_dt_bias, ssm_d, ssm_out_norm, w_out, ffn2_norm, ffn2_w_gate, ffn2_w_up, ffn2_w_down, ple_norm, ple_w_gate, ple_w_proj, ple_post_norm, final_norm):
    bsz, seq, d = x.shape
    n = bsz * seq
    depth = p.shape[0]
    o_z = 4 * GDN_WIDTH + 2 * GDN_HEADS
    o_xbc = o_z + SSM_WIDTH
    o_dt = o_xbc + SSM_WIDTH + 2 * SSM_GROUPS * SSM_STATE
    expand = (jnp.arange(LANE)[:, None] == (jnp.arange(SSM_WIDTH)[None, :] // SSM_HEAD_DIM)).astype(BF16)
    ffn_tm, ffn1_tf, ffn2_tf = 1024, 256, 512

    h = x.reshape(n, d)
    for i in range(depth):
        h1, hn1, wg1, wu1, wd1 = _ffn(
            h, ffn1_norm[i], ffn1_w_gate[i], ffn1_w_up[i], ffn1_w_down[i], mix_norm[i],
            tm=ffn_tm, tf=ffn1_tf, x_single_buffer=True, first_tile_only=True)
        wi_t = jnp.swapaxes(w_in[i], 0, 1)
        jobs = _cast_jobs((n // ffn_tm, ffn1_w_gate.shape[2] // ffn1_tf),
                          [ffn2_w_gate[i], ffn2_w_up[i], ffn2_w_down[i], w_out[i], ple_w_gate[i], wi_t])
        h, hn, wg2, wu2, wd2, wo, wpg, wi_t = _ffn(
            h, ffn1_norm[i], wg1, wu1, wd1, mix_norm[i], jobs, tm=ffn_tm, tf=ffn1_tf, resume_from=(h1, hn1))
        qkv, xbc = 3 * GDN_WIDTH, o_dt - o_xbc
        raw_cols = GDN_WIDTH + LANE
        pad = jnp.zeros((LANE - SSM_HEADS, d), BF16)
        w_raw_t = jnp.concatenate([wi_t[qkv:qkv + raw_cols], wi_t[o_z:o_xbc], wi_t[o_dt:], pad], axis=0)
        gdn_act = _proj_conv(hn, wi_t, gdn_conv_w[i], jnp.zeros((qkv,), F32), cols=qkv, seq=seq, tn=qkv // 3)
        ssd_act = _proj_conv(hn, wi_t[o_xbc:o_dt], ssm_conv_w[i], ssm_conv_b[i], cols=xbc, seq=seq, tn=xbc // 2)
        raw = _matmul_nt(hn, w_raw_t, lo=0, cols=2 * raw_cols, tn=raw_cols).reshape(bsz, seq, -1)
        o_gdn = _gdn_mixer(gdn_act.reshape(bsz, seq, -1), raw, 0, _lane_row(gdn_a_log[i]),
                           _lane_row(gdn_dt_bias[i]), gdn_out_norm[i])
        o_ssm = _ssd_mixer(ssd_act.reshape(bsz, seq, -1), raw, 1,
                           _lane_row(ssm_a_log[i]), _lane_row(ssm_dt_bias[i]),
                           jnp.repeat(ssm_d[i].astype(F32), SSM_HEAD_DIM).reshape(1, SSM_WIDTH),
                           ssm_out_norm[i], expand)
        h = _out_proj(h, o_gdn.reshape(n, GDN_WIDTH), o_ssm.reshape(n, SSM_WIDTH), wo)
        h, hn = _ffn(h, ffn2_norm[i], wg2, wu2, wd2, ple_norm[i], tm=ffn_tm, tf=ffn2_tf, x_single_buffer=True)
        assert i == depth - 1, "depth > 1 would need an un-normalised variant of the embedding kernel"
        h = _ple(h, hn, p[i].reshape(n, -1), wpg, ple_w_proj[i].astype(BF16), ple_post_norm[i], final_norm)
    return h.reshape(bsz, seq, d)
```

```python
import functools

import jax
import jax.numpy as jnp
from jax import lax
from jax.experimental import pallas as pl
from jax.experimental.pallas import tpu as pltpu

F32 = jnp.float32
BF16 = jnp.bfloat16
EPS = 1e-6

LANE = 128
CONV_K = 4
CONV_HALO = 8
MIX_CHUNK = 128
GDN_HEADS = 8
GDN_HEAD_DIM = 128
GDN_WIDTH = GDN_HEADS * GDN_HEAD_DIM
GDN_HEAD_GROUP = 8
SSM_HEADS = 16
SSM_HEAD_DIM = 64
SSM_WIDTH = SSM_HEADS * SSM_HEAD_DIM
SSM_GROUPS = 2
SSM_STATE = 128
SSM_GROUP_WIDTH = SSM_WIDTH // SSM_GROUPS
VMEM_LIMIT = 60 * 1024 * 1024


def _cparams(sem):
    return pltpu.CompilerParams(dimension_semantics=sem, vmem_limit_bytes=VMEM_LIMIT)


def _dot(a, b):
    return jnp.dot(a, b, preferred_element_type=F32)


def _dot_nt(a, b):
    return lax.dot_general(a, b, (((1,), (1,)), ((), ())), preferred_element_type=F32)


def _bdot(a, b):
    return _dot(a.astype(BF16), b.astype(BF16))


def _split3(a):
    a1 = a.astype(BF16)
    r1 = a - a1.astype(F32)
    a2 = r1.astype(BF16)
    a3 = (r1 - a2.astype(F32)).astype(BF16)
    return a1, a2, a3


def _dot_sel_lhs(sel, b):
    b1, b2, b3 = _split3(b)
    return _dot(sel, b1) + (_dot(sel, b2) + _dot(sel, b3))


def _dot_sel_rhs(a, sel):
    a1, a2, a3 = _split3(a)
    return _dot(a1, sel) + (_dot(a2, sel) + _dot(a3, sel))


def _sigmoid(x):
    return 1.0 / (1.0 + jnp.exp(-x))


def _silu(x):
    return x * _sigmoid(x)


def _softplus(x):
    return jnp.maximum(x, 0.0) + jnp.log1p(jnp.exp(-jnp.abs(x)))


def _rms(x):
    return x * lax.rsqrt(jnp.mean(x * x, axis=-1, keepdims=True) + EPS)


def _ffn_kernel(*refs, n_cast, slabs, emit_weights, skip_first_tile):
    it = iter(refs)
    x_ref, nw_ref, wg_ref, wu_ref, wd_ref, nnw_ref = (next(it) for _ in range(6))
    cast_in = [next(it) for _ in range(n_cast)]
    o_ref, on_ref = next(it), next(it)
    wb_refs = [next(it) for _ in range(3)] if emit_weights else ()
    cast_out = [next(it) for _ in range(n_cast)]
    xn_ref = next(it)
    j = pl.program_id(1)
    last = pl.num_programs(1) - 1
    tm = x_ref.shape[0]
    slab_rows = [pl.ds(r * (tm // slabs), tm // slabs) for r in range(slabs)]

    def active(cond):
        return cond & (pl.program_id(0) >= 1) if skip_first_tile else cond

    if skip_first_tile:
        @pl.when((pl.program_id(0) == 0) & (j == 0))
        def _():
            o_ref[...] = jnp.zeros_like(o_ref)
            on_ref[...] = jnp.zeros_like(on_ref)

    def weights():
        if not emit_weights:
            return wg_ref[...], wu_ref[...], wd_ref[...]
        ws = [r[...].astype(BF16) for r in (wg_ref, wu_ref, wd_ref)]
        for dst, w in zip(wb_refs, ws):
            dst[...] = w
        return ws

    for src, dst in zip(cast_in, cast_out):
        dst[...] = src[...].astype(BF16)

    def step(rows, first, final, w):
        wg, wu, wd = w
        if first:
            base = x_ref[rows, :]
            xn = (_rms(base) * nw_ref[...]).astype(BF16)
            xn_ref[rows, :] = xn
        else:
            base = o_ref[rows, :]
            xn = xn_ref[rows, :]
        g = _dot(xn, wg)
        u = _dot(xn, wu)
        h = (0.5 * _silu(g) * u).astype(BF16)
        out = base + _dot(h, wd)
        o_ref[rows, :] = out
        if final:
            on_ref[rows, :] = (_rms(out) * nnw_ref[...]).astype(BF16)

    @pl.when(active(j == 0))
    def _():
        w = weights()
        for rows in slab_rows:
            step(rows, True, False, w)

    @pl.when(active((j > 0) & (j < last)))
    def _():
        step(slice(None), False, False, weights())

    @pl.when(active(j == last))
    def _():
        w = weights()
        for rows in slab_rows:
            step(rows, False, True, w)


def _ffn(x, norm_w, wg, wu, wd, next_norm_w, cast_jobs=(), *, tm, tf, slabs=2, x_single_buffer=False,
         first_tile_only=False, skip_first_tile=False):
    n, d = x.shape
    f = wg.shape[1]
    row_tiles = 1 if first_tile_only else n // tm
    grid = (row_tiles, f // tf)
    assert grid[1] >= 2 and not (first_tile_only and (cast_jobs or skip_first_tile))
    cast_specs = [pl.BlockSpec(blk, imap) for _, blk, imap in cast_jobs]
    vec_spec = pl.BlockSpec((1, d), lambda i, j: (0, 0))
    row_spec = pl.BlockSpec((tm, d), lambda i, j: (i, 0))
    if skip_first_tile:
        w_col = lambda i, j: (0, jnp.where(i == 0, 0, j))
        w_row = lambda i, j: (jnp.where(i == 0, 0, j), 0)
    else:
        w_col = lambda i, j: (0, j)
        w_row = lambda i, j: (j, 0)
    x_spec = pl.BlockSpec((tm, d), lambda i, j: (i, 0), pipeline_mode=pl.Buffered(1)) if x_single_buffer else row_spec
    w_specs = [pl.BlockSpec((d, tf), w_col), pl.BlockSpec((d, tf), w_col), pl.BlockSpec((tf, d), w_row)]
    return pl.pallas_call(
        functools.partial(_ffn_kernel, n_cast=len(cast_jobs), slabs=slabs, emit_weights=first_tile_only,
                          skip_first_tile=skip_first_tile),
        grid=grid,
        in_specs=[x_spec, vec_spec] + w_specs + [vec_spec] + cast_specs,
        out_specs=[row_spec, row_spec] + (w_specs if first_tile_only else []) + cast_specs,
        out_shape=[jax.ShapeDtypeStruct((row_tiles * tm, d), F32), jax.ShapeDtypeStruct((row_tiles * tm, d), BF16)]
        + ([jax.ShapeDtypeStruct(w.shape, BF16) for w in (wg, wu, wd)] if first_tile_only else [])
        + [jax.ShapeDtypeStruct(job[0].shape, BF16) for job in cast_jobs],
        scratch_shapes=[pltpu.VMEM((tm, d), BF16)],
        compiler_params=_cparams(("parallel", "arbitrary")),
        name="ffn",
    )(x, norm_w.reshape(1, d), wg, wu, wd, next_norm_w.reshape(1, d), *[job[0] for job in cast_jobs])


def _cast_jobs(grid, mats):
    gi, gj = grid
    jobs = []
    for a in mats:
        r, c = a.shape
        if r % (gi * 8) == 0 and c % (gj * LANE) == 0:
            jobs.append((a, (r // gi, c // gj), lambda i, j: (i, j)))
        elif r % (gj * 8) == 0 and c % (gi * LANE) == 0:
            jobs.append((a, (r // gj, c // gi), lambda i, j: (j, i)))
        else:
            nb = max(k for k in range(1, gj + 1) if c % (k * LANE) == 0)
            assert r % (gi * 8) == 0
            jobs.append((a, (r // gi, c // nb), lambda i, j, nb=nb: (i, jnp.minimum(j, nb - 1))))
    return jobs


def _matmul_nt_kernel(x_ref, w_ref, o_ref):
    o_ref[...] = _dot_nt(x_ref[...], w_ref[...])


def _matmul_nt(xn, wt, *, cols=None, tm=1024, tn):
    n, d = xn.shape
    cols = wt.shape[0] if cols is None else cols
    assert cols % tn == 0 and cols <= wt.shape[0]
    return pl.pallas_call(
        _matmul_nt_kernel,
        grid=(cols // tn, n // tm),
        in_specs=[
            pl.BlockSpec((tm, d), lambda j, i: (i, 0)),
            pl.BlockSpec((tn, d), lambda j, i: (j, 0)),
        ],
        out_specs=pl.BlockSpec((tm, tn), lambda j, i: (i, j)),
        out_shape=jax.ShapeDtypeStruct((n, cols), F32),
        compiler_params=_cparams(("parallel", "parallel")),
        name="in_proj",
    )(xn, wt)


def _causal_conv(halo, cur, w_ref, w_lo):
    c, width = cur.shape
    xx = jnp.concatenate([halo, cur], axis=0)
    y = w_ref[CONV_K - 1:CONV_K, w_lo:w_lo + width] * cur
    for j in range(CONV_K - 1):
        shifted = pltpu.roll(xx, CONV_K - 1 - j, axis=0)
        y = y + w_ref[j:j + 1, w_lo:w_lo + width] * shifted[CONV_HALO:CONV_HALO + c]
    return y


def _halo_rows(cur_ref, prev_ref, r0, cols, use_prev):
    if r0 == 0:
        return prev_ref[:, cols] * use_prev
    return cur_ref[r0 - CONV_HALO:r0, cols]


def _inv_unit_lower_many(mats, row, col):
    c = mats[0].shape[0]
    base = 16
    diag_blk = (row // base) == (col // base)
    eye = jnp.where(row == col, 1.0, 0.0)
    ps = [jnp.where(diag_blk, -a, 0.0) for a in mats]
    ts = [eye + p for p in ps]
    for _ in range(3):
        ps = [_bdot(p, p) for p in ps]
        ts = [t + _bdot(t, p) for t, p in zip(ts, ps)]
    b = 2 * base
    while b <= c:
        off = ((row // b) == (col // b)) & ((row // (b // 2)) != (col // (b // 2)))
        ets = [_bdot(jnp.where(off, a, 0.0), t) for a, t in zip(mats, ts)]
        ts = [t - _bdot(t, et) for t, et in zip(ts, ets)]
        b *= 2
    return ts


def _gdn_kernel(cur_ref, prev_ref, cw_ref, alog_ref, dtb_ref, onw_ref, o_ref, s_ref):
    t_idx = pl.program_id(1)

    @pl.when(t_idx == 0)
    def _():
        s_ref[...] = jnp.zeros_like(s_ref)

    c = MIX_CHUNK
    hd = GDN_HEAD_DIM
    use_prev = jnp.where(t_idx > 0, 1.0, 0.0)
    row = lax.broadcasted_iota(jnp.int32, (c, c), 0)
    col = lax.broadcasted_iota(jnp.int32, (c, c), 1)
    for r0 in range(0, cur_ref.shape[0], c):
        _gdn_chunk(r0, cur_ref, prev_ref, cw_ref, alog_ref, dtb_ref, onw_ref, o_ref, s_ref, use_prev, row, col)


def _gdn_chunk(r0, cur_ref, prev_ref, cw_ref, alog_ref, dtb_ref, onw_ref, o_ref, s_ref, use_prev, row, col):
    c = MIX_CHUNK
    hd = GDN_HEAD_DIM
    rows = slice(r0, r0 + c)
    causal = row >= col
    strict = row > col
    tril = jnp.where(causal, 1.0, 0.0).astype(BF16)

    small = cur_ref[rows, 4 * GDN_WIDTH:4 * GDN_WIDTH + LANE]
    g = -jnp.exp(alog_ref[...]) * _softplus(small + dtb_ref[...])
    beta = _sigmoid(small)
    gc = _dot_sel_lhs(tril, g)
    gc_t = gc.T
    g_last = gc[c - 1:c, :]
    e_gc = jnp.exp(gc)
    e_rest = jnp.exp(g_last - gc)
    g_tot = jnp.exp(g_last)

    def conv(lo):
        cols = slice(lo, lo + hd)
        halo = _halo_rows(cur_ref, prev_ref, r0, cols, use_prev)
        return _silu(_causal_conv(halo, cur_ref[rows, cols], cw_ref, lo))

    def l2norm(x):
        return x * lax.rsqrt(jnp.sum(x * x, axis=-1, keepdims=True) + EPS)

    for g0 in range(0, GDN_HEADS, GDN_HEAD_GROUP):
        heads = range(g0, g0 + GDN_HEAD_GROUP)
        q = {h: l2norm(conv(h * hd)) * (hd ** -0.5) for h in heads}
        k = {h: l2norm(conv(GDN_WIDTH + h * hd)) for h in heads}
        v = {h: conv(2 * GDN_WIDTH + h * hd) for h in heads}
        beta_c = {h: beta[:, GDN_HEADS + h:GDN_HEADS + h + 1] for h in heads}
        decay = {h: jnp.exp(jnp.where(causal, gc[:, h:h + 1] - gc_t[h:h + 1, :], -jnp.inf)) for h in heads}
        kb = {h: k[h] * beta_c[h] for h in heads}
        k_b = {h: k[h].astype(BF16) for h in heads}
        a = {h: jnp.where(strict, _dot_nt(kb[h].astype(BF16), k_b[h]) * decay[h], 0.0) for h in heads}
        qk = {h: (_dot_nt(q[h].astype(BF16), k_b[h]) * decay[h]).astype(BF16) for h in heads}
        t_inv = dict(zip(heads, _inv_unit_lower_many([a[h] for h in heads], row, col)))
        uw = {h: _bdot(t_inv[h], jnp.concatenate([v[h] * beta_c[h], kb[h] * e_gc[:, h:h + 1]], axis=1))
              for h in heads}
        s = {h: s_ref[h] for h in heads}
        wq = {h: jnp.concatenate([uw[h][:, hd:], q[h] * e_gc[:, h:h + 1]], axis=0) for h in heads}
        wq_s = {h: _bdot(wq[h], s[h]) for h in heads}
        v_new = {h: (uw[h][:, :hd] - wq_s[h][:c]).astype(BF16) for h in heads}
        o = {h: wq_s[h][c:] + _dot(qk[h], v_new[h]) for h in heads}
        k_dec_t = {h: (k[h] * e_rest[:, h:h + 1]).T.astype(BF16) for h in heads}
        for h in heads:
            s_ref[h] = s[h] * g_tot[:, h:h + 1] + _dot(k_dec_t[h], v_new[h])
        for h in heads:
            gate = cur_ref[rows, 3 * GDN_WIDTH + h * hd:3 * GDN_WIDTH + (h + 1) * hd]
            o_ref[rows, h * hd:(h + 1) * hd] = (_rms(o[h]) * onw_ref[...] * _silu(gate)).astype(o_ref.dtype)


def _gdn_mixer(proj, conv_w, a_log_row, dt_bias_row, out_norm_w, *, chunks_per_block=2):
    b, t, width = proj.shape
    c = chunks_per_block * MIX_CHUNK
    qkv = 3 * GDN_WIDTH
    halo_blocks = c // CONV_HALO
    return pl.pallas_call(
        _gdn_kernel,
        grid=(b, t // c),
        in_specs=[
            pl.BlockSpec((None, c, width), lambda i, j: (i, j, 0)),
            pl.BlockSpec((None, CONV_HALO, qkv), lambda i, j: (i, jnp.maximum(j * halo_blocks - 1, 0), 0)),
            pl.BlockSpec((CONV_K, qkv), lambda i, j: (0, 0)),
            pl.BlockSpec((1, LANE), lambda i, j: (0, 0)),
            pl.BlockSpec((1, LANE), lambda i, j: (0, 0)),
            pl.BlockSpec((1, GDN_HEAD_DIM), lambda i, j: (0, 0)),
        ],
        out_specs=pl.BlockSpec((None, c, GDN_WIDTH), lambda i, j: (i, j, 0)),
        out_shape=jax.ShapeDtypeStruct((b, t, GDN_WIDTH), BF16),
        scratch_shapes=[pltpu.VMEM((GDN_HEADS, GDN_HEAD_DIM, GDN_HEAD_DIM), F32)],
        compiler_params=_cparams(("parallel", "arbitrary")),
        name="gdn_mixer",
    )(proj, proj, conv_w, a_log_row, dt_bias_row, out_norm_w.reshape(1, GDN_HEAD_DIM))


def _ssd_kernel(cur_ref, prev_ref, cw_ref, cb_ref, alog_ref, dtb_ref, dskip_ref, onw_ref, expand_ref,
                o_ref, st_ref):
    t_idx = pl.program_id(1)

    @pl.when(t_idx == 0)
    def _():
        st_ref[...] = jnp.zeros_like(st_ref)

    c = MIX_CHUNK
    hp = SSM_HEAD_DIM
    gw = SSM_GROUP_WIDTH
    z_lo = 0
    x_lo = SSM_WIDTH
    small_lo = x_lo + SSM_WIDTH + 2 * SSM_GROUPS * SSM_STATE
    use_prev = jnp.where(t_idx > 0, 1.0, 0.0)
    row = lax.broadcasted_iota(jnp.int32, (c, c), 0)
    col = lax.broadcasted_iota(jnp.int32, (c, c), 1)
    causal = row >= col
    tril = jnp.where(causal, 1.0, 0.0).astype(BF16)
    lane = lax.broadcasted_iota(jnp.int32, (c, LANE), 1)
    first_head = lane < hp
    expand = expand_ref[...]

    for r0 in range(0, cur_ref.shape[0], c):
        rows = slice(r0, r0 + c)
        small = cur_ref[rows, small_lo:small_lo + LANE]
        dt = _softplus(small + dtb_ref[...])
        acs = _dot_sel_lhs(tril, dt * -jnp.exp(alog_ref[...]))
        acs_t = acs.T
        dt_w = _dot_sel_rhs(dt, expand)
        acs_w = _dot_sel_rhs(acs, expand)
        e_acs_w = jnp.exp(acs_w)
        e_rest_w = jnp.exp(acs_w[c - 1:c, :] - acs_w)

        def conv(lo, width, r0=r0, rows=rows):
            cols = slice(x_lo + lo, x_lo + lo + width)
            halo = _halo_rows(cur_ref, prev_ref, r0, cols, use_prev)
            return _silu(_causal_conv(halo, cur_ref[rows, cols], cw_ref, lo) + cb_ref[:, lo:lo + width])

        for g in range(SSM_GROUPS):
            bg = conv(SSM_WIDTH + g * SSM_STATE, SSM_STATE)
            cg = conv(SSM_WIDTH + (SSM_GROUPS + g) * SSM_STATE, SSM_STATE)
            bg_b = bg.astype(BF16)
            cg_b = cg.astype(BF16)
            cb = _dot_nt(cg_b, bg_b)
            cols = slice(g * gw, (g + 1) * gw)
            xs = conv(g * gw, gw)
            xc = xs * dt_w[:, cols]
            state = st_ref[g]
            y = _dot(cg_b, state.astype(BF16)) * e_acs_w[:, cols]
            chunk_state = _dot(bg.T.astype(BF16), (xc * e_rest_w[:, cols]).astype(BF16))
            st_ref[g] = state * e_acs_w[c - 1:c, cols] + chunk_state
            pieces = []
            for pr in range(gw // LANE):
                xp = xc[:, pr * LANE:(pr + 1) * LANE]
                lhs = []
                for r in range(2):
                    hidx = g * (SSM_HEADS // SSM_GROUPS) + 2 * pr + r
                    seg = jnp.exp(jnp.where(causal, acs[:, hidx:hidx + 1] - acs_t[hidx:hidx + 1, :], -jnp.inf))
                    lhs.append((cb * seg).astype(BF16))
                rhs = jnp.concatenate([jnp.where(first_head, xp, 0.0), jnp.where(first_head, 0.0, xp)], axis=0)
                pieces.append(_dot(jnp.concatenate(lhs, axis=1), rhs.astype(BF16)))
            y = y + jnp.concatenate(pieces, axis=1) + xs * dskip_ref[:, cols]
            y = y * _silu(cur_ref[rows, z_lo + g * gw:z_lo + (g + 1) * gw])
            o_ref[rows, cols] = (_rms(y) * onw_ref[:, cols]).astype(o_ref.dtype)


def _ssd_mixer(proj, conv_w, conv_b, a_log_row, dt_bias_row, d_skip_w, out_norm_w, expand, *, chunks_per_block=2):
    b, t, width = proj.shape
    c = chunks_per_block * MIX_CHUNK
    xbc = SSM_WIDTH + 2 * SSM_GROUPS * SSM_STATE
    halo_blocks = c // CONV_HALO
    const = lambda i, j: (0, 0)
    return pl.pallas_call(
        _ssd_kernel,
        grid=(b, t // c),
        in_specs=[
            pl.BlockSpec((None, c, width), lambda i, j: (i, j, 0)),
            pl.BlockSpec((None, CONV_HALO, width), lambda i, j: (i, jnp.maximum(j * halo_blocks - 1, 0), 0)),
            pl.BlockSpec((CONV_K, xbc), const),
            pl.BlockSpec((1, xbc), const),
            pl.BlockSpec((1, LANE), const),
            pl.BlockSpec((1, LANE), const),
            pl.BlockSpec((1, SSM_WIDTH), const),
            pl.BlockSpec((1, SSM_WIDTH), const),
            pl.BlockSpec((LANE, SSM_WIDTH), const),
        ],
        out_specs=pl.BlockSpec((None, c, SSM_WIDTH), lambda i, j: (i, j, 0)),
        out_shape=jax.ShapeDtypeStruct((b, t, SSM_WIDTH), BF16),
        scratch_shapes=[pltpu.VMEM((SSM_GROUPS, SSM_STATE, SSM_GROUP_WIDTH), F32)],
        compiler_params=_cparams(("parallel", "arbitrary")),
        name="ssd_mixer",
    )(proj, proj, conv_w, conv_b.reshape(1, xbc), a_log_row, dt_bias_row, d_skip_w, out_norm_w.reshape(1, SSM_WIDTH),
      expand)


def _out_proj_kernel(h_ref, og_ref, os_ref, wg_ref, ws_ref, o_ref):
    o_ref[...] = h_ref[...] + _dot(og_ref[...], wg_ref[...]) + _dot(os_ref[...], ws_ref[...])


def _out_proj(h, o_gdn, o_ssm, w, *, tm=512):
    n, d = h.shape
    assert GDN_WIDTH == SSM_WIDTH
    return pl.pallas_call(
        _out_proj_kernel,
        grid=(n // tm,),
        in_specs=[
            pl.BlockSpec((tm, d), lambda i: (i, 0)),
            pl.BlockSpec((tm, GDN_WIDTH), lambda i: (i, 0)),
            pl.BlockSpec((tm, SSM_WIDTH), lambda i: (i, 0)),
            pl.BlockSpec((GDN_WIDTH, d), lambda i: (0, 0)),
            pl.BlockSpec((SSM_WIDTH, d), lambda i: (1, 0)),
        ],
        out_specs=pl.BlockSpec((tm, d), lambda i: (i, 0)),
        out_shape=jax.ShapeDtypeStruct((n, d), F32),
        compiler_params=_cparams(("parallel",)),
        name="out_proj",
    )(h, o_gdn, o_ssm, w, w)


def _ple_kernel(h_ref, hn_ref, p_ref, wg_ref, wp_ref, pnw_ref, fnw_ref, o_ref, *, slabs):
    for lo, hi in zip(slabs[:-1], slabs[1:]):
        rows = pl.ds(lo, hi - lo)
        gate = _sigmoid(_dot(hn_ref[rows, :], wg_ref[...]))
        emb = _rms(_dot(p_ref[rows, :].astype(BF16), wp_ref[...])) * pnw_ref[...]
        o_ref[rows, :] = _rms(h_ref[rows, :] + gate * emb) * fnw_ref[...]


def _ple(h, hn, p, w_gate, w_proj, post_norm_w, final_norm_w, *, tm=512, slabs=(0, 256, 512)):
    n, d = h.shape
    pd = p.shape[1]
    const = lambda i: (0, 0)
    return pl.pallas_call(
        functools.partial(_ple_kernel, slabs=slabs),
        grid=(n // tm,),
        in_specs=[
            pl.BlockSpec((tm, d), lambda i: (i, 0)),
            pl.BlockSpec((tm, d), lambda i: (i, 0)),
            pl.BlockSpec((tm, pd), lambda i: (i, 0)),
            pl.BlockSpec((d, d), const),
            pl.BlockSpec((pd, d), const),
            pl.BlockSpec((1, d), const),
            pl.BlockSpec((1, d), const),
        ],
        out_specs=pl.BlockSpec((tm, d), lambda i: (i, 0)),
        out_shape=jax.ShapeDtypeStruct((n, d), F32),
        compiler_params=_cparams(("parallel",)),
        name="ple_final",
    )(h, hn, p, w_gate, w_proj, post_norm_w.reshape(1, d), final_norm_w.reshape(1, d))


def _lane_row(v):
    return jnp.zeros((1, LANE), F32).at[0, :v.shape[0]].set(v.astype(F32))


def kernel(x, p, ffn1_norm, ffn1_w_gate, ffn1_w_up, ffn1_w_down, mix_norm, w_in, gdn_conv_w, gdn_a_log, gdn_dt_bias, gdn_out_norm, ssm_conv_w, ssm_conv_b, ssm_a_log, ssm_dt_bias, ssm_d, ssm_out_norm, w_out, ffn2_norm, ffn2_w_gate, ffn2_w_up, ffn2_w_down, ple_norm, ple_w_gate, ple_w_proj, ple_post_norm, final_norm):
    bsz, seq, d = x.shape
    n = bsz * seq
    depth = p.shape[0]
    o_z = 4 * GDN_WIDTH + 2 * GDN_HEADS
    gdn_cols = 4 * GDN_WIDTH + LANE
    expand = (jnp.arange(LANE)[:, None] == (jnp.arange(SSM_WIDTH)[None, :] // SSM_HEAD_DIM)).astype(BF16)
    ffn_tm, ffn1_tf, ffn2_tf = 1024, 256, 512

    h = x.reshape(n, d)
    for i in range(depth):
        h1, hn1, wg1, wu1, wd1 = _ffn(
            h, ffn1_norm[i], ffn1_w_gate[i], ffn1_w_up[i], ffn1_w_down[i], mix_norm[i],
            tm=ffn_tm, tf=ffn1_tf, x_single_buffer=True, first_tile_only=True)
        wi_t = jnp.swapaxes(w_in[i], 0, 1)
        jobs = _cast_jobs((n // ffn_tm, ffn1_w_gate.shape[2] // ffn1_tf),
                          [ffn2_w_gate[i], ffn2_w_up[i], ffn2_w_down[i], w_out[i], ple_w_gate[i], wi_t])
        h, hn, wg2, wu2, wd2, wo, wpg, wi_t = _ffn(
            h, ffn1_norm[i], wg1, wu1, wd1, mix_norm[i], jobs, tm=ffn_tm, tf=ffn1_tf, skip_first_tile=True)
        h = lax.dynamic_update_slice(h, h1, (0, 0))
        hn = lax.dynamic_update_slice(hn, hn1, (0, 0))
        pad = jnp.zeros((LANE - SSM_HEADS, d), BF16)
        w_ssd_t = jnp.concatenate([wi_t[o_z:], pad], axis=0)
        proj_gdn = _matmul_nt(hn, wi_t, cols=gdn_cols, tn=gdn_cols // 3)
        proj_ssd = _matmul_nt(hn, w_ssd_t, tn=w_ssd_t.shape[0] // 3)
        o_gdn = _gdn_mixer(proj_gdn.reshape(bsz, seq, -1), gdn_conv_w[i], _lane_row(gdn_a_log[i]),
                           _lane_row(gdn_dt_bias[i]), gdn_out_norm[i])
        o_ssm = _ssd_mixer(proj_ssd.reshape(bsz, seq, -1), ssm_conv_w[i], ssm_conv_b[i],
                           _lane_row(ssm_a_log[i]), _lane_row(ssm_dt_bias[i]),
                           jnp.repeat(ssm_d[i].astype(F32), SSM_HEAD_DIM).reshape(1, SSM_WIDTH),
                           ssm_out_norm[i], expand)
        h = _out_proj(h, o_gdn.reshape(n, GDN_WIDTH), o_ssm.reshape(n, SSM_WIDTH), wo)
        h, hn = _ffn(h, ffn2_norm[i], wg2, wu2, wd2, ple_norm[i], tm=ffn_tm, tf=ffn2_tf, x_single_buffer=True)
        assert i == depth - 1, "depth > 1 would need an un-normalised variant of the embedding kernel"
        h = _ple(h, hn, p[i].reshape(n, -1), wpg, ple_w_proj[i].astype(BF16), ple_post_norm[i], final_norm)
    return h.reshape(bsz, seq, d)
```

```python
import functools

import jax
import jax.numpy as jnp
from jax import lax
from jax.experimental import pallas as pl
from jax.experimental.pallas import tpu as pltpu

F32 = jnp.float32
BF16 = jnp.bfloat16
EPS = 1e-6

LANE = 128
CONV_K = 4
CONV_HALO = 8
MIX_CHUNK = 128
GDN_HEADS = 8
GDN_HEAD_DIM = 128
GDN_WIDTH = GDN_HEADS * GDN_HEAD_DIM
GDN_HEAD_GROUP = 8
SSM_HEADS = 16
SSM_HEAD_DIM = 64
SSM_WIDTH = SSM_HEADS * SSM_HEAD_DIM
SSM_GROUPS = 2
SSM_STATE = 128
SSM_GROUP_WIDTH = SSM_WIDTH // SSM_GROUPS
VMEM_LIMIT = 60 * 1024 * 1024


def _cparams(sem):
    return pltpu.CompilerParams(dimension_semantics=sem, vmem_limit_bytes=VMEM_LIMIT)


def _dot(a, b):
    return jnp.dot(a, b, preferred_element_type=F32)


def _dot_nt(a, b):
    return lax.dot_general(a, b, (((1,), (1,)), ((), ())), preferred_element_type=F32)


def _bdot(a, b):
    return _dot(a.astype(BF16), b.astype(BF16))


def _split3(a):
    a1 = a.astype(BF16)
    r1 = a - a1.astype(F32)
    a2 = r1.astype(BF16)
    a3 = (r1 - a2.astype(F32)).astype(BF16)
    return a1, a2, a3


def _dot_sel_lhs(sel, b):
    b1, b2, b3 = _split3(b)
    return _dot(sel, b1) + (_dot(sel, b2) + _dot(sel, b3))


def _dot_sel_rhs(a, sel):
    a1, a2, a3 = _split3(a)
    return _dot(a1, sel) + (_dot(a2, sel) + _dot(a3, sel))


def _sigmoid(x):
    return 1.0 / (1.0 + jnp.exp(-x))


def _silu(x):
    return x * _sigmoid(x)


def _softplus(x):
    return jnp.maximum(x, 0.0) + jnp.log1p(jnp.exp(-jnp.abs(x)))


def _rms(x):
    return x * lax.rsqrt(jnp.mean(x * x, axis=-1, keepdims=True) + EPS)


def _ffn_kernel(*refs, n_cast, slabs, emit_weights, skip_first_tile):
    it = iter(refs)
    x_ref, nw_ref, wg_ref, wu_ref, wd_ref, nnw_ref = (next(it) for _ in range(6))
    cast_in = [next(it) for _ in range(n_cast)]
    o_ref, on_ref = next(it), next(it)
    wb_refs = [next(it) for _ in range(3)] if emit_weights else ()
    cast_out = [next(it) for _ in range(n_cast)]
    xn_ref = next(it)
    j = pl.program_id(1)
    last = pl.num_programs(1) - 1
    tm = x_ref.shape[0]
    slab_rows = [pl.ds(r * (tm // slabs), tm // slabs) for r in range(slabs)]

    def active(cond):
        return cond & (pl.program_id(0) >= 1) if skip_first_tile else cond

    if skip_first_tile:
        @pl.when((pl.program_id(0) == 0) & (j == 0))
        def _():
            o_ref[...] = jnp.zeros_like(o_ref)
            on_ref[...] = jnp.zeros_like(on_ref)

    def weights():
        if not emit_weights:
            return wg_ref[...], wu_ref[...], wd_ref[...]
        ws = [r[...].astype(BF16) for r in (wg_ref, wu_ref, wd_ref)]
        for dst, w in zip(wb_refs, ws):
            dst[...] = w
        return ws

    for src, dst in zip(cast_in, cast_out):
        dst[...] = src[...].astype(BF16)

    def step(rows, first, final, w):
        wg, wu, wd = w
        if first:
            base = x_ref[rows, :]
            xn = (_rms(base) * nw_ref[...]).astype(BF16)
            xn_ref[rows, :] = xn
        else:
            base = o_ref[rows, :]
            xn = xn_ref[rows, :]
        g = _dot(xn, wg)
        u = _dot(xn, wu)
        h = (0.5 * _silu(g) * u).astype(BF16)
        out = base + _dot(h, wd)
        o_ref[rows, :] = out
        if final:
            on_ref[rows, :] = (_rms(out) * nnw_ref[...]).astype(BF16)

    @pl.when(active(j == 0))
    def _():
        w = weights()
        for rows in slab_rows:
            step(rows, True, False, w)

    @pl.when(active((j > 0) & (j < last)))
    def _():
        step(slice(None), False, False, weights())

    @pl.when(active(j == last))
    def _():
        w = weights()
        for rows in slab_rows:
            step(rows, False, True, w)


def _ffn(x, norm_w, wg, wu, wd, next_norm_w, cast_jobs=(), *, tm, tf, slabs=2, x_single_buffer=False,
         on_single_buffer=False, first_tile_only=False, skip_first_tile=False):
    n, d = x.shape
    f = wg.shape[1]
    row_tiles = 1 if first_tile_only else n // tm
    grid = (row_tiles, f // tf)
    assert grid[1] >= 2 and not (first_tile_only and (cast_jobs or skip_first_tile))
    cast_specs = [pl.BlockSpec(blk, imap) for _, blk, imap in cast_jobs]
    vec_spec = pl.BlockSpec((1, d), lambda i, j: (0, 0))
    row_spec = pl.BlockSpec((tm, d), lambda i, j: (i, 0))
    if skip_first_tile:
        w_col = lambda i, j: (0, jnp.where(i == 0, 0, j))
        w_row = lambda i, j: (jnp.where(i == 0, 0, j), 0)
    else:
        w_col = lambda i, j: (0, j)
        w_row = lambda i, j: (j, 0)
    x_spec = pl.BlockSpec((tm, d), lambda i, j: (i, 0), pipeline_mode=pl.Buffered(1)) if x_single_buffer else row_spec
    on_spec = pl.BlockSpec((tm, d), lambda i, j: (i, 0), pipeline_mode=pl.Buffered(1)) if on_single_buffer else row_spec
    w_specs = [pl.BlockSpec((d, tf), w_col), pl.BlockSpec((d, tf), w_col), pl.BlockSpec((tf, d), w_row)]
    return pl.pallas_call(
        functools.partial(_ffn_kernel, n_cast=len(cast_jobs), slabs=slabs, emit_weights=first_tile_only,
                          skip_first_tile=skip_first_tile),
        grid=grid,
        in_specs=[x_spec, vec_spec] + w_specs + [vec_spec] + cast_specs,
        out_specs=[row_spec, on_spec] + (w_specs if first_tile_only else []) + cast_specs,
        out_shape=[jax.ShapeDtypeStruct((row_tiles * tm, d), F32), jax.ShapeDtypeStruct((row_tiles * tm, d), BF16)]
        + ([jax.ShapeDtypeStruct(w.shape, BF16) for w in (wg, wu, wd)] if first_tile_only else [])
        + [jax.ShapeDtypeStruct(job[0].shape, BF16) for job in cast_jobs],
        scratch_shapes=[pltpu.VMEM((tm, d), BF16)],
        compiler_params=_cparams(("parallel", "arbitrary")),
        name="ffn",
    )(x, norm_w.reshape(1, d), wg, wu, wd, next_norm_w.reshape(1, d), *[job[0] for job in cast_jobs])


def _cast_jobs(grid, mats):
    gi, gj = grid
    jobs = []
    for a in mats:
        r, c = a.shape
        if r % (gi * 8) == 0 and c % (gj * LANE) == 0:
            jobs.append((a, (r // gi, c // gj), lambda i, j: (i, j)))
        elif r % (gj * 8) == 0 and c % (gi * LANE) == 0:
            jobs.append((a, (r // gj, c // gi), lambda i, j: (j, i)))
        else:
            nb = max(k for k in range(1, gj + 1) if c % (k * LANE) == 0)
            assert r % (gi * 8) == 0
            jobs.append((a, (r // gi, c // nb), lambda i, j, nb=nb: (i, jnp.minimum(j, nb - 1))))
    return jobs


def _matmul_nt_kernel(x_ref, w_ref, o_ref):
    o_ref[...] = _dot_nt(x_ref[...], w_ref[...])


def _matmul_nt(xn, wt, *, cols=None, tm=1024, tn):
    n, d = xn.shape
    cols = wt.shape[0] if cols is None else cols
    assert cols % tn == 0 and cols <= wt.shape[0]
    return pl.pallas_call(
        _matmul_nt_kernel,
        grid=(cols // tn, n // tm),
        in_specs=[
            pl.BlockSpec((tm, d), lambda j, i: (i, 0)),
            pl.BlockSpec((tn, d), lambda j, i: (j, 0)),
        ],
        out_specs=pl.BlockSpec((tm, tn), lambda j, i: (i, j)),
        out_shape=jax.ShapeDtypeStruct((n, cols), F32),
        compiler_params=_cparams(("parallel", "parallel")),
        name="in_proj",
    )(xn, wt)


def _causal_conv(halo, cur, w_ref, w_lo):
    c, width = cur.shape
    xx = jnp.concatenate([halo, cur], axis=0)
    y = w_ref[CONV_K - 1:CONV_K, w_lo:w_lo + width] * cur
    for j in range(CONV_K - 1):
        shifted = pltpu.roll(xx, CONV_K - 1 - j, axis=0)
        y = y + w_ref[j:j + 1, w_lo:w_lo + width] * shifted[CONV_HALO:CONV_HALO + c]
    return y


def _halo_rows(cur_ref, prev_ref, r0, cols, use_prev):
    if r0 == 0:
        return prev_ref[:, cols] * use_prev
    return cur_ref[r0 - CONV_HALO:r0, cols]


def _inv_unit_lower_many(mats, row, col):
    c = mats[0].shape[0]
    base = 16
    diag_blk = (row // base) == (col // base)
    eye = jnp.where(row == col, 1.0, 0.0)
    ps = [jnp.where(diag_blk, -a, 0.0) for a in mats]
    ts = [eye + p for p in ps]
    for _ in range(3):
        ps = [_bdot(p, p) for p in ps]
        ts = [t + _bdot(t, p) for t, p in zip(ts, ps)]
    b = 2 * base
    while b <= c:
        off = ((row // b) == (col // b)) & ((row // (b // 2)) != (col // (b // 2)))
        ets = [_bdot(jnp.where(off, a, 0.0), t) for a, t in zip(mats, ts)]
        ts = [t - _bdot(t, et) for t, et in zip(ts, ets)]
        b *= 2
    return ts


def _gdn_kernel(cur_ref, prev_ref, cw_ref, alog_ref, dtb_ref, onw_ref, o_ref, s_ref):
    t_idx = pl.program_id(1)

    @pl.when(t_idx == 0)
    def _():
        s_ref[...] = jnp.zeros_like(s_ref)

    c = MIX_CHUNK
    hd = GDN_HEAD_DIM
    use_prev = jnp.where(t_idx > 0, 1.0, 0.0)
    row = lax.broadcasted_iota(jnp.int32, (c, c), 0)
    col = lax.broadcasted_iota(jnp.int32, (c, c), 1)
    for r0 in range(0, cur_ref.shape[0], c):
        _gdn_chunk(r0, cur_ref, prev_ref, cw_ref, alog_ref, dtb_ref, onw_ref, o_ref, s_ref, use_prev, row, col)


def _gdn_chunk(r0, cur_ref, prev_ref, cw_ref, alog_ref, dtb_ref, onw_ref, o_ref, s_ref, use_prev, row, col):
    c = MIX_CHUNK
    hd = GDN_HEAD_DIM
    rows = slice(r0, r0 + c)
    causal = row >= col
    strict = row > col
    tril = jnp.where(causal, 1.0, 0.0).astype(BF16)

    small = cur_ref[rows, 4 * GDN_WIDTH:4 * GDN_WIDTH + LANE]
    g = -jnp.exp(alog_ref[...]) * _softplus(small + dtb_ref[...])
    beta = _sigmoid(small)
    gc = _dot_sel_lhs(tril, g)
    gc_t = gc.T
    g_last = gc[c - 1:c, :]
    e_gc = jnp.exp(gc)
    e_rest = jnp.exp(g_last - gc)
    g_tot = jnp.exp(g_last)

    def conv(lo):
        cols = slice(lo, lo + hd)
        halo = _halo_rows(cur_ref, prev_ref, r0, cols, use_prev)
        return _silu(_causal_conv(halo, cur_ref[rows, cols], cw_ref, lo))

    def l2norm(x):
        return x * lax.rsqrt(jnp.sum(x * x, axis=-1, keepdims=True) + EPS)

    for g0 in range(0, GDN_HEADS, GDN_HEAD_GROUP):
        heads = range(g0, g0 + GDN_HEAD_GROUP)
        q = {h: l2norm(conv(h * hd)) * (hd ** -0.5) for h in heads}
        k = {h: l2norm(conv(GDN_WIDTH + h * hd)) for h in heads}
        v = {h: conv(2 * GDN_WIDTH + h * hd) for h in heads}
        beta_c = {h: beta[:, GDN_HEADS + h:GDN_HEADS + h + 1] for h in heads}
        decay = {h: jnp.exp(jnp.where(causal, gc[:, h:h + 1] - gc_t[h:h + 1, :], -jnp.inf)) for h in heads}
        kb = {h: k[h] * beta_c[h] for h in heads}
        k_b = {h: k[h].astype(BF16) for h in heads}
        a = {h: jnp.where(strict, _dot_nt(kb[h].astype(BF16), k_b[h]) * decay[h], 0.0) for h in heads}
        qk = {h: (_dot_nt(q[h].astype(BF16), k_b[h]) * decay[h]).astype(BF16) for h in heads}
        t_inv = dict(zip(heads, _inv_unit_lower_many([a[h] for h in heads], row, col)))
        uw = {h: _bdot(t_inv[h], jnp.concatenate([v[h] * beta_c[h], kb[h] * e_gc[:, h:h + 1]], axis=1))
              for h in heads}
        s = {h: s_ref[h] for h in heads}
        wq = {h: jnp.concatenate([uw[h][:, hd:], q[h] * e_gc[:, h:h + 1]], axis=0) for h in heads}
        wq_s = {h: _bdot(wq[h], s[h]) for h in heads}
        v_new = {h: (uw[h][:, :hd] - wq_s[h][:c]).astype(BF16) for h in heads}
        o = {h: wq_s[h][c:] + _dot(qk[h], v_new[h]) for h in heads}
        k_dec_t = {h: (k[h] * e_rest[:, h:h + 1]).T.astype(BF16) for h in heads}
        for h in heads:
            s_ref[h] = s[h] * g_tot[:, h:h + 1] + _dot(k_dec_t[h], v_new[h])
        for h in heads:
            gate = cur_ref[rows, 3 * GDN_WIDTH + h * hd:3 * GDN_WIDTH + (h + 1) * hd]
            o_ref[rows, h * hd:(h + 1) * hd] = (_rms(o[h]) * onw_ref[...] * _silu(gate)).astype(o_ref.dtype)


def _gdn_mixer(proj, conv_w, a_log_row, dt_bias_row, out_norm_w, *, chunks_per_block=2):
    b, t, width = proj.shape
    c = chunks_per_block * MIX_CHUNK
    qkv = 3 * GDN_WIDTH
    halo_blocks = c // CONV_HALO
    return pl.pallas_call(
        _gdn_kernel,
        grid=(b, t // c),
        in_specs=[
            pl.BlockSpec((None, c, width), lambda i, j: (i, j, 0)),
            pl.BlockSpec((None, CONV_HALO, qkv), lambda i, j: (i, jnp.maximum(j * halo_blocks - 1, 0), 0)),
            pl.BlockSpec((CONV_K, qkv), lambda i, j: (0, 0)),
            pl.BlockSpec((1, LANE), lambda i, j: (0, 0)),
            pl.BlockSpec((1, LANE), lambda i, j: (0, 0)),
            pl.BlockSpec((1, GDN_HEAD_DIM), lambda i, j: (0, 0)),
        ],
        out_specs=pl.BlockSpec((None, c, GDN_WIDTH), lambda i, j: (i, j, 0)),
        out_shape=jax.ShapeDtypeStruct((b, t, GDN_WIDTH), BF16),
        scratch_shapes=[pltpu.VMEM((GDN_HEADS, GDN_HEAD_DIM, GDN_HEAD_DIM), F32)],
        compiler_params=_cparams(("parallel", "arbitrary")),
        name="gdn_mixer",
    )(proj, proj, conv_w, a_log_row, dt_bias_row, out_norm_w.reshape(1, GDN_HEAD_DIM))


def _ssd_kernel(cur_ref, prev_ref, cw_ref, cb_ref, alog_ref, dtb_ref, dskip_ref, onw_ref, expand_ref,
                o_ref, st_ref):
    t_idx = pl.program_id(1)

    @pl.when(t_idx == 0)
    def _():
        st_ref[...] = jnp.zeros_like(st_ref)

    c = MIX_CHUNK
    hp = SSM_HEAD_DIM
    gw = SSM_GROUP_WIDTH
    z_lo = 0
    x_lo = SSM_WIDTH
    small_lo = x_lo + SSM_WIDTH + 2 * SSM_GROUPS * SSM_STATE
    use_prev = jnp.where(t_idx > 0, 1.0, 0.0)
    row = lax.broadcasted_iota(jnp.int32, (c, c), 0)
    col = lax.broadcasted_iota(jnp.int32, (c, c), 1)
    causal = row >= col
    tril = jnp.where(causal, 1.0, 0.0).astype(BF16)
    lane = lax.broadcasted_iota(jnp.int32, (c, LANE), 1)
    first_head = lane < hp
    expand = expand_ref[...]

    for r0 in range(0, cur_ref.shape[0], c):
        rows = slice(r0, r0 + c)
        small = cur_ref[rows, small_lo:small_lo + LANE]
        dt = _softplus(small + dtb_ref[...])
        acs = _dot_sel_lhs(tril, dt * -jnp.exp(alog_ref[...]))
        acs_t = acs.T
        dt_w = _dot_sel_rhs(dt, expand)
        acs_w = _dot_sel_rhs(acs, expand)
        e_acs_w = jnp.exp(acs_w)
        e_rest_w = jnp.exp(acs_w[c - 1:c, :] - acs_w)

        def conv(lo, width, r0=r0, rows=rows):
            cols = slice(x_lo + lo, x_lo + lo + width)
            halo = _halo_rows(cur_ref, prev_ref, r0, cols, use_prev)
            return _silu(_causal_conv(halo, cur_ref[rows, cols], cw_ref, lo) + cb_ref[:, lo:lo + width])

        for g in range(SSM_GROUPS):
            bg = conv(SSM_WIDTH + g * SSM_STATE, SSM_STATE)
            cg = conv(SSM_WIDTH + (SSM_GROUPS + g) * SSM_STATE, SSM_STATE)
            bg_b = bg.astype(BF16)
            cg_b = cg.astype(BF16)
            cb = _dot_nt(cg_b, bg_b)
            cols = slice(g * gw, (g + 1) * gw)
            xs = conv(g * gw, gw)
            xc = xs * dt_w[:, cols]
            state = st_ref[g]
            y = _dot(cg_b, state.astype(BF16)) * e_acs_w[:, cols]
            chunk_state = _dot(bg.T.astype(BF16), (xc * e_rest_w[:, cols]).astype(BF16))
            st_ref[g] = state * e_acs_w[c - 1:c, cols] + chunk_state
            pieces = []
            for pr in range(gw // LANE):
                xp = xc[:, pr * LANE:(pr + 1) * LANE]
                lhs = []
                for r in range(2):
                    hidx = g * (SSM_HEADS // SSM_GROUPS) + 2 * pr + r
                    seg = jnp.exp(jnp.where(causal, acs[:, hidx:hidx + 1] - acs_t[hidx:hidx + 1, :], -jnp.inf))
                    lhs.append((cb * seg).astype(BF16))
                rhs = jnp.concatenate([jnp.where(first_head, xp, 0.0), jnp.where(first_head, 0.0, xp)], axis=0)
                pieces.append(_dot(jnp.concatenate(lhs, axis=1), rhs.astype(BF16)))
            y = y + jnp.concatenate(pieces, axis=1) + xs * dskip_ref[:, cols]
            y = y * _silu(cur_ref[rows, z_lo + g * gw:z_lo + (g + 1) * gw])
            o_ref[rows, cols] = (_rms(y) * onw_ref[:, cols]).astype(o_ref.dtype)


def _ssd_mixer(proj, conv_w, conv_b, a_log_row, dt_bias_row, d_skip_w, out_norm_w, expand, *, chunks_per_block=2):
    b, t, width = proj.shape
    c = chunks_per_block * MIX_CHUNK
    xbc = SSM_WIDTH + 2 * SSM_GROUPS * SSM_STATE
    halo_blocks = c // CONV_HALO
    const = lambda i, j: (0, 0)
    return pl.pallas_call(
        _ssd_kernel,
        grid=(b, t // c),
        in_specs=[
            pl.BlockSpec((None, c, width), lambda i, j: (i, j, 0)),
            pl.BlockSpec((None, CONV_HALO, width), lambda i, j: (i, jnp.maximum(j * halo_blocks - 1, 0), 0)),
            pl.BlockSpec((CONV_K, xbc), const),
            pl.BlockSpec((1, xbc), const),
            pl.BlockSpec((1, LANE), const),
            pl.BlockSpec((1, LANE), const),
            pl.BlockSpec((1, SSM_WIDTH), const),
            pl.BlockSpec((1, SSM_WIDTH), const),
            pl.BlockSpec((LANE, SSM_WIDTH), const),
        ],
        out_specs=pl.BlockSpec((None, c, SSM_WIDTH), lambda i, j: (i, j, 0)),
        out_shape=jax.ShapeDtypeStruct((b, t, SSM_WIDTH), BF16),
        scratch_shapes=[pltpu.VMEM((SSM_GROUPS, SSM_STATE, SSM_GROUP_WIDTH), F32)],
        compiler_params=_cparams(("parallel", "arbitrary")),
        name="ssd_mixer",
    )(proj, proj, conv_w, conv_b.reshape(1, xbc), a_log_row, dt_bias_row, d_skip_w, out_norm_w.reshape(1, SSM_WIDTH),
      expand)


def _out_proj_kernel(h_ref, og_ref, os_ref, wg_ref, ws_ref, o_ref):
    o_ref[...] = h_ref[...] + _dot(og_ref[...], wg_ref[...]) + _dot(os_ref[...], ws_ref[...])


def _out_proj(h, o_gdn, o_ssm, w, *, tm=512):
    n, d = h.shape
    assert GDN_WIDTH == SSM_WIDTH
    return pl.pallas_call(
        _out_proj_kernel,
        grid=(n // tm,),
        in_specs=[
            pl.BlockSpec((tm, d), lambda i: (i, 0)),
            pl.BlockSpec((tm, GDN_WIDTH), lambda i: (i, 0)),
            pl.BlockSpec((tm, SSM_WIDTH), lambda i: (i, 0)),
            pl.BlockSpec((GDN_WIDTH, d), lambda i: (0, 0)),
            pl.BlockSpec((SSM_WIDTH, d), lambda i: (1, 0)),
        ],
        out_specs=pl.BlockSpec((tm, d), lambda i: (i, 0)),
        out_shape=jax.ShapeDtypeStruct((n, d), F32),
        compiler_params=_cparams(("parallel",)),
        name="out_proj",
    )(h, o_gdn, o_ssm, w, w)


def _ple_kernel(h_ref, hn_ref, p_ref, wg_ref, wp_ref, pnw_ref, fnw_ref, o_ref, *, slabs):
    for lo, hi in zip(slabs[:-1], slabs[1:]):
        rows = pl.ds(lo, hi - lo)
        gate = _sigmoid(_dot(hn_ref[rows, :], wg_ref[...]))
        emb = _rms(_dot(p_ref[rows, :].astype(BF16), wp_ref[...])) * pnw_ref[...]
        o_ref[rows, :] = _rms(h_ref[rows, :] + gate * emb) * fnw_ref[...]


def _ple(h, hn, p, w_gate, w_proj, post_norm_w, final_norm_w, *, tm=512, slabs=(0, 256, 512)):
    n, d = h.shape
    pd = p.shape[1]
    const = lambda i: (0, 0)
    return pl.pallas_call(
        functools.partial(_ple_kernel, slabs=slabs),
        grid=(n // tm,),
        in_specs=[
            pl.BlockSpec((tm, d), lambda i: (i, 0)),
            pl.BlockSpec((tm, d), lambda i: (i, 0)),
            pl.BlockSpec((tm, pd), lambda i: (i, 0)),
            pl.BlockSpec((d, d), const),
            pl.BlockSpec((pd, d), const),
            pl.BlockSpec((1, d), const),
            pl.BlockSpec((1, d), const),
        ],
        out_specs=pl.BlockSpec((tm, d), lambda i: (i, 0)),
        out_shape=jax.ShapeDtypeStruct((n, d), F32),
        compiler_params=_cparams(("parallel",)),
        name="ple_final",
    )(h, hn, p, w_gate, w_proj, post_norm_w.reshape(1, d), final_norm_w.reshape(1, d))


def _lane_row(v):
    return jnp.zeros((1, LANE), F32).at[0, :v.shape[0]].set(v.astype(F32))


def kernel(x, p, ffn1_norm, ffn1_w_gate, ffn1_w_up, ffn1_w_down, mix_norm, w_in, gdn_conv_w, gdn_a_log, gdn_dt_bias, gdn_out_norm, ssm_conv_w, ssm_conv_b, ssm_a_log, ssm_dt_bias, ssm_d, ssm_out_norm, w_out, ffn2_norm, ffn2_w_gate, ffn2_w_up, ffn2_w_down, ple_norm, ple_w_gate, ple_w_proj, ple_post_norm, final_norm):
    bsz, seq, d = x.shape
    n = bsz * seq
    depth = p.shape[0]
    o_z = 4 * GDN_WIDTH + 2 * GDN_HEADS
    gdn_cols = 4 * GDN_WIDTH + LANE
    expand = (jnp.arange(LANE)[:, None] == (jnp.arange(SSM_WIDTH)[None, :] // SSM_HEAD_DIM)).astype(BF16)
    ffn_tm, ffn1_tf, ffn2_tf = 1024, 256, 512

    h = x.reshape(n, d)
    for i in range(depth):
        h1, hn1, wg1, wu1, wd1 = _ffn(
            h, ffn1_norm[i], ffn1_w_gate[i], ffn1_w_up[i], ffn1_w_down[i], mix_norm[i],
            tm=ffn_tm, tf=ffn1_tf, x_single_buffer=True, first_tile_only=True)
        wi_t = jnp.swapaxes(w_in[i], 0, 1)
        jobs = _cast_jobs((n // ffn_tm, ffn1_w_gate.shape[2] // ffn1_tf),
                          [ffn2_w_gate[i], ffn2_w_up[i], ffn2_w_down[i], w_out[i], ple_w_gate[i], wi_t])
        h, hn, wg2, wu2, wd2, wo, wpg, wi_t = _ffn(
            h, ffn1_norm[i], wg1, wu1, wd1, mix_norm[i], jobs, tm=ffn_tm, tf=ffn1_tf, skip_first_tile=True)
        h = lax.dynamic_update_slice(h, h1, (0, 0))
        hn = lax.dynamic_update_slice(hn, hn1, (0, 0))
        pad = jnp.zeros((LANE - SSM_HEADS, d), BF16)
        w_ssd_t = jnp.concatenate([wi_t[o_z:], pad], axis=0)
        proj_gdn = _matmul_nt(hn, wi_t, cols=gdn_cols, tn=gdn_cols // 3)
        proj_ssd = _matmul_nt(hn, w_ssd_t, tn=w_ssd_t.shape[0] // 3)
        o_gdn = _gdn_mixer(proj_gdn.reshape(bsz, seq, -1), gdn_conv_w[i], _lane_row(gdn_a_log[i]),
                           _lane_row(gdn_dt_bias[i]), gdn_out_norm[i])
        o_ssm = _ssd_mixer(proj_ssd.reshape(bsz, seq, -1), ssm_conv_w[i], ssm_conv_b[i],
                           _lane_row(ssm_a_log[i]), _lane_row(ssm_dt_bias[i]),
                           jnp.repeat(ssm_d[i].astype(F32), SSM_HEAD_DIM).reshape(1, SSM_WIDTH),
                           ssm_out_norm[i], expand)
        h = _out_proj(h, o_gdn.reshape(n, GDN_WIDTH), o_ssm.reshape(n, SSM_WIDTH), wo)
        h, hn = _ffn(h, ffn2_norm[i], wg2, wu2, wd2, ple_norm[i], tm=ffn_tm, tf=ffn2_tf, on_single_buffer=True)
        assert i == depth - 1, "depth > 1 would need an un-normalised variant of the embedding kernel"
        h = _ple(h, hn, p[i].reshape(n, -1), wpg, ple_w_proj[i].astype(BF16), ple_post_norm[i], final_norm)
    return h.reshape(bsz, seq, d)
```

```python
import functools

import jax
import jax.numpy as jnp
from jax import lax
from jax.experimental import pallas as pl
from jax.experimental.pallas import tpu as pltpu

F32 = jnp.float32
BF16 = jnp.bfloat16
EPS = 1e-6

LANE = 128
CONV_K = 4
CONV_HALO = 8
MIX_CHUNK = 128
GDN_HEADS = 8
GDN_HEAD_DIM = 128
GDN_WIDTH = GDN_HEADS * GDN_HEAD_DIM
GDN_HEAD_GROUP = 8
SSM_HEADS = 16
SSM_HEAD_DIM = 64
SSM_WIDTH = SSM_HEADS * SSM_HEAD_DIM
SSM_GROUPS = 2
SSM_STATE = 128
SSM_GROUP_WIDTH = SSM_WIDTH // SSM_GROUPS
VMEM_LIMIT = 60 * 1024 * 1024


def _cparams(sem):
    return pltpu.CompilerParams(dimension_semantics=sem, vmem_limit_bytes=VMEM_LIMIT)


def _dot(a, b):
    return jnp.dot(a, b, preferred_element_type=F32)


def _dot_nt(a, b):
    return lax.dot_general(a, b, (((1,), (1,)), ((), ())), preferred_element_type=F32)


def _bdot(a, b):
    return _dot(a.astype(BF16), b.astype(BF16))


def _split3(a):
    a1 = a.astype(BF16)
    r1 = a - a1.astype(F32)
    a2 = r1.astype(BF16)
    a3 = (r1 - a2.astype(F32)).astype(BF16)
    return a1, a2, a3


def _dot_sel_lhs(sel, b):
    b1, b2, b3 = _split3(b)
    return _dot(sel, b1) + (_dot(sel, b2) + _dot(sel, b3))


def _dot_sel_rhs(a, sel):
    a1, a2, a3 = _split3(a)
    return _dot(a1, sel) + (_dot(a2, sel) + _dot(a3, sel))


def _sigmoid(x):
    return 1.0 / (1.0 + jnp.exp(-x))


def _silu(x):
    return x * _sigmoid(x)


def _softplus(x):
    return jnp.maximum(x, 0.0) + jnp.log1p(jnp.exp(-jnp.abs(x)))


def _rms(x):
    return x * lax.rsqrt(jnp.mean(x * x, axis=-1, keepdims=True) + EPS)


def _ffn_kernel(*refs, n_cast, slabs, emit_weights, skip_first_tile):
    it = iter(refs)
    x_ref, nw_ref, wg_ref, wu_ref, wd_ref, nnw_ref = (next(it) for _ in range(6))
    cast_in = [next(it) for _ in range(n_cast)]
    o_ref, on_ref = next(it), next(it)
    wb_refs = [next(it) for _ in range(3)] if emit_weights else ()
    cast_out = [next(it) for _ in range(n_cast)]
    xn_ref = next(it)
    j = pl.program_id(1)
    last = pl.num_programs(1) - 1
    tm = x_ref.shape[0]
    slab_rows = [pl.ds(r * (tm // slabs), tm // slabs) for r in range(slabs)]

    def active(cond):
        return cond & (pl.program_id(0) >= 1) if skip_first_tile else cond

    if skip_first_tile:
        @pl.when((pl.program_id(0) == 0) & (j == 0))
        def _():
            o_ref[...] = jnp.zeros_like(o_ref)
            on_ref[...] = jnp.zeros_like(on_ref)

    def weights():
        if not emit_weights:
            return wg_ref[...], wu_ref[...], wd_ref[...]
        ws = [r[...].astype(BF16) for r in (wg_ref, wu_ref, wd_ref)]
        for dst, w in zip(wb_refs, ws):
            dst[...] = w
        return ws

    for src, dst in zip(cast_in, cast_out):
        dst[...] = src[...].astype(BF16)

    def step(rows, first, final, w):
        wg, wu, wd = w
        if first:
            base = x_ref[rows, :]
            xn = (_rms(base) * nw_ref[...]).astype(BF16)
            xn_ref[rows, :] = xn
        else:
            base = o_ref[rows, :]
            xn = xn_ref[rows, :]
        g = _dot(xn, wg)
        u = _dot(xn, wu)
        h = (0.5 * _silu(g) * u).astype(BF16)
        out = base + _dot(h, wd)
        o_ref[rows, :] = out
        if final:
            on_ref[rows, :] = (_rms(out) * nnw_ref[...]).astype(BF16)

    @pl.when(active(j == 0))
    def _():
        w = weights()
        for rows in slab_rows:
            step(rows, True, False, w)

    @pl.when(active((j > 0) & (j < last)))
    def _():
        step(slice(None), False, False, weights())

    @pl.when(active(j == last))
    def _():
        w = weights()
        for rows in slab_rows:
            step(rows, False, True, w)


def _ffn(x, norm_w, wg, wu, wd, next_norm_w, cast_jobs=(), *, tm, tf, slabs=2, x_single_buffer=False,
         on_single_buffer=False, first_tile_only=False, skip_first_tile=False):
    n, d = x.shape
    f = wg.shape[1]
    row_tiles = 1 if first_tile_only else n // tm
    grid = (row_tiles, f // tf)
    assert grid[1] >= 2 and not (first_tile_only and (cast_jobs or skip_first_tile))
    cast_specs = [pl.BlockSpec(blk, imap) for _, blk, imap in cast_jobs]
    vec_spec = pl.BlockSpec((1, d), lambda i, j: (0, 0))
    row_spec = pl.BlockSpec((tm, d), lambda i, j: (i, 0))
    if skip_first_tile:
        w_col = lambda i, j: (0, jnp.where(i == 0, 0, j))
        w_row = lambda i, j: (jnp.where(i == 0, 0, j), 0)
    else:
        w_col = lambda i, j: (0, j)
        w_row = lambda i, j: (j, 0)
    x_spec = pl.BlockSpec((tm, d), lambda i, j: (i, 0), pipeline_mode=pl.Buffered(1)) if x_single_buffer else row_spec
    on_spec = pl.BlockSpec((tm, d), lambda i, j: (i, 0), pipeline_mode=pl.Buffered(1)) if on_single_buffer else row_spec
    w_specs = [pl.BlockSpec((d, tf), w_col), pl.BlockSpec((d, tf), w_col), pl.BlockSpec((tf, d), w_row)]
    return pl.pallas_call(
        functools.partial(_ffn_kernel, n_cast=len(cast_jobs), slabs=slabs, emit_weights=first_tile_only,
                          skip_first_tile=skip_first_tile),
        grid=grid,
        in_specs=[x_spec, vec_spec] + w_specs + [vec_spec] + cast_specs,
        out_specs=[row_spec, on_spec] + (w_specs if first_tile_only else []) + cast_specs,
        out_shape=[jax.ShapeDtypeStruct((row_tiles * tm, d), F32), jax.ShapeDtypeStruct((row_tiles * tm, d), BF16)]
        + ([jax.ShapeDtypeStruct(w.shape, BF16) for w in (wg, wu, wd)] if first_tile_only else [])
        + [jax.ShapeDtypeStruct(job[0].shape, BF16) for job in cast_jobs],
        scratch_shapes=[pltpu.VMEM((tm, d), BF16)],
        compiler_params=_cparams(("parallel", "arbitrary")),
        name="ffn",
    )(x, norm_w.reshape(1, d), wg, wu, wd, next_norm_w.reshape(1, d), *[job[0] for job in cast_jobs])


def _cast_jobs(grid, mats):
    gi, gj = grid
    jobs = []
    for a in mats:
        r, c = a.shape
        if r % (gi * 8) == 0 and c % (gj * LANE) == 0:
            jobs.append((a, (r // gi, c // gj), lambda i, j: (i, j)))
        elif r % (gj * 8) == 0 and c % (gi * LANE) == 0:
            jobs.append((a, (r // gj, c // gi), lambda i, j: (j, i)))
        else:
            nb = max(k for k in range(1, gj + 1) if c % (k * LANE) == 0)
            assert r % (gi * 8) == 0
            jobs.append((a, (r // gi, c // nb), lambda i, j, nb=nb: (i, jnp.minimum(j, nb - 1))))
    return jobs


def _matmul_nt_kernel(head_ref, x_ref, w_ref, o_ref, *, head_tiles):
    @pl.when(pl.program_id(1) < head_tiles)
    def _():
        o_ref[...] = _dot_nt(head_ref[...], w_ref[...])

    @pl.when(pl.program_id(1) >= head_tiles)
    def _():
        o_ref[...] = _dot_nt(x_ref[...], w_ref[...])


def _matmul_nt(x_head, xn, wt, *, cols=None, tm=1024, tn):
    n, d = xn.shape
    cols = wt.shape[0] if cols is None else cols
    head_tiles = x_head.shape[0] // tm
    assert cols % tn == 0 and cols <= wt.shape[0] and head_tiles >= 1 and x_head.shape[0] % tm == 0
    return pl.pallas_call(
        functools.partial(_matmul_nt_kernel, head_tiles=head_tiles),
        grid=(cols // tn, n // tm),
        in_specs=[
            pl.BlockSpec((tm, d), lambda j, i: (jnp.minimum(i, head_tiles - 1), 0)),
            pl.BlockSpec((tm, d), lambda j, i: (i, 0)),
            pl.BlockSpec((tn, d), lambda j, i: (j, 0)),
        ],
        out_specs=pl.BlockSpec((tm, tn), lambda j, i: (i, j)),
        out_shape=jax.ShapeDtypeStruct((n, cols), F32),
        compiler_params=_cparams(("parallel", "parallel")),
        name="in_proj",
    )(x_head, xn, wt)


def _causal_conv(halo, cur, w_ref, w_lo):
    c, width = cur.shape
    xx = jnp.concatenate([halo, cur], axis=0)
    y = w_ref[CONV_K - 1:CONV_K, w_lo:w_lo + width] * cur
    for j in range(CONV_K - 1):
        shifted = pltpu.roll(xx, CONV_K - 1 - j, axis=0)
        y = y + w_ref[j:j + 1, w_lo:w_lo + width] * shifted[CONV_HALO:CONV_HALO + c]
    return y


def _halo_rows(cur_ref, prev_ref, r0, cols, use_prev):
    if r0 == 0:
        return prev_ref[:, cols] * use_prev
    return cur_ref[r0 - CONV_HALO:r0, cols]


def _inv_unit_lower_many(mats, row, col):
    c = mats[0].shape[0]
    base = 16
    diag_blk = (row // base) == (col // base)
    eye = jnp.where(row == col, 1.0, 0.0)
    ps = [jnp.where(diag_blk, -a, 0.0) for a in mats]
    ts = [eye + p for p in ps]
    for _ in range(3):
        ps = [_bdot(p, p) for p in ps]
        ts = [t + _bdot(t, p) for t, p in zip(ts, ps)]
    b = 2 * base
    while b <= c:
        off = ((row // b) == (col // b)) & ((row // (b // 2)) != (col // (b // 2)))
        ets = [_bdot(jnp.where(off, a, 0.0), t) for a, t in zip(mats, ts)]
        ts = [t - _bdot(t, et) for t, et in zip(ts, ets)]
        b *= 2
    return ts


def _gdn_kernel(cur_ref, prev_ref, cw_ref, alog_ref, dtb_ref, onw_ref, o_ref, s_ref):
    t_idx = pl.program_id(1)

    @pl.when(t_idx == 0)
    def _():
        s_ref[...] = jnp.zeros_like(s_ref)

    c = MIX_CHUNK
    hd = GDN_HEAD_DIM
    use_prev = jnp.where(t_idx > 0, 1.0, 0.0)
    row = lax.broadcasted_iota(jnp.int32, (c, c), 0)
    col = lax.broadcasted_iota(jnp.int32, (c, c), 1)
    for r0 in range(0, cur_ref.shape[0], c):
        _gdn_chunk(r0, cur_ref, prev_ref, cw_ref, alog_ref, dtb_ref, onw_ref, o_ref, s_ref, use_prev, row, col)


def _gdn_chunk(r0, cur_ref, prev_ref, cw_ref, alog_ref, dtb_ref, onw_ref, o_ref, s_ref, use_prev, row, col):
    c = MIX_CHUNK
    hd = GDN_HEAD_DIM
    rows = slice(r0, r0 + c)
    causal = row >= col
    strict = row > col
    tril = jnp.where(causal, 1.0, 0.0).astype(BF16)

    small = cur_ref[rows, 4 * GDN_WIDTH:4 * GDN_WIDTH + LANE]
    g = -jnp.exp(alog_ref[...]) * _softplus(small + dtb_ref[...])
    beta = _sigmoid(small)
    gc = _dot_sel_lhs(tril, g)
    gc_t = gc.T
    g_last = gc[c - 1:c, :]
    e_gc = jnp.exp(gc)
    e_rest = jnp.exp(g_last - gc)
    g_tot = jnp.exp(g_last)

    def conv(lo):
        cols = slice(lo, lo + hd)
        halo = _halo_rows(cur_ref, prev_ref, r0, cols, use_prev)
        return _silu(_causal_conv(halo, cur_ref[rows, cols], cw_ref, lo))

    def l2norm(x):
        return x * lax.rsqrt(jnp.sum(x * x, axis=-1, keepdims=True) + EPS)

    for g0 in range(0, GDN_HEADS, GDN_HEAD_GROUP):
        heads = range(g0, g0 + GDN_HEAD_GROUP)
        q = {h: l2norm(conv(h * hd)) * (hd ** -0.5) for h in heads}
        k = {h: l2norm(conv(GDN_WIDTH + h * hd)) for h in heads}
        v = {h: conv(2 * GDN_WIDTH + h * hd) for h in heads}
        beta_c = {h: beta[:, GDN_HEADS + h:GDN_HEADS + h + 1] for h in heads}
        decay = {h: jnp.exp(jnp.where(causal, gc[:, h:h + 1] - gc_t[h:h + 1, :], -jnp.inf)) for h in heads}
        kb = {h: k[h] * beta_c[h] for h in heads}
        k_b = {h: k[h].astype(BF16) for h in heads}
        a = {h: jnp.where(strict, _dot_nt(kb[h].astype(BF16), k_b[h]) * decay[h], 0.0) for h in heads}
        qk = {h: (_dot_nt(q[h].astype(BF16), k_b[h]) * decay[h]).astype(BF16) for h in heads}
        t_inv = dict(zip(heads, _inv_unit_lower_many([a[h] for h in heads], row, col)))
        uw = {h: _bdot(t_inv[h], jnp.concatenate([v[h] * beta_c[h], kb[h] * e_gc[:, h:h + 1]], axis=1))
              for h in heads}
        s = {h: s_ref[h] for h in heads}
        wq = {h: jnp.concatenate([uw[h][:, hd:], q[h] * e_gc[:, h:h + 1]], axis=0) for h in heads}
        wq_s = {h: _bdot(wq[h], s[h]) for h in heads}
        v_new = {h: (uw[h][:, :hd] - wq_s[h][:c]).astype(BF16) for h in heads}
        o = {h: wq_s[h][c:] + _dot(qk[h], v_new[h]) for h in heads}
        k_dec_t = {h: (k[h] * e_rest[:, h:h + 1]).T.astype(BF16) for h in heads}
        for h in heads:
            s_ref[h] = s[h] * g_tot[:, h:h + 1] + _dot(k_dec_t[h], v_new[h])
        for h in heads:
            gate = cur_ref[rows, 3 * GDN_WIDTH + h * hd:3 * GDN_WIDTH + (h + 1) * hd]
            o_ref[rows, h * hd:(h + 1) * hd] = (_rms(o[h]) * onw_ref[...] * _silu(gate)).astype(o_ref.dtype)


def _gdn_mixer(proj, conv_w, a_log_row, dt_bias_row, out_norm_w, *, chunks_per_block=2):
    b, t, width = proj.shape
    c = chunks_per_block * MIX_CHUNK
    qkv = 3 * GDN_WIDTH
    halo_blocks = c // CONV_HALO
    return pl.pallas_call(
        _gdn_kernel,
        grid=(b, t // c),
        in_specs=[
            pl.BlockSpec((None, c, width), lambda i, j: (i, j, 0)),
            pl.BlockSpec((None, CONV_HALO, qkv), lambda i, j: (i, jnp.maximum(j * halo_blocks - 1, 0), 0)),
            pl.BlockSpec((CONV_K, qkv), lambda i, j: (0, 0)),
            pl.BlockSpec((1, LANE), lambda i, j: (0, 0)),
            pl.BlockSpec((1, LANE), lambda i, j: (0, 0)),
            pl.BlockSpec((1, GDN_HEAD_DIM), lambda i, j: (0, 0)),
        ],
        out_specs=pl.BlockSpec((None, c, GDN_WIDTH), lambda i, j: (i, j, 0)),
        out_shape=jax.ShapeDtypeStruct((b, t, GDN_WIDTH), BF16),
        scratch_shapes=[pltpu.VMEM((GDN_HEADS, GDN_HEAD_DIM, GDN_HEAD_DIM), F32)],
        compiler_params=_cparams(("parallel", "arbitrary")),
        name="gdn_mixer",
    )(proj, proj, conv_w, a_log_row, dt_bias_row, out_norm_w.reshape(1, GDN_HEAD_DIM))


def _ssd_kernel(cur_ref, prev_ref, cw_ref, cb_ref, alog_ref, dtb_ref, dskip_ref, onw_ref, expand_ref,
                o_ref, st_ref):
    t_idx = pl.program_id(1)

    @pl.when(t_idx == 0)
    def _():
        st_ref[...] = jnp.zeros_like(st_ref)

    c = MIX_CHUNK
    hp = SSM_HEAD_DIM
    gw = SSM_GROUP_WIDTH
    z_lo = 0
    x_lo = SSM_WIDTH
    small_lo = x_lo + SSM_WIDTH + 2 * SSM_GROUPS * SSM_STATE
    use_prev = jnp.where(t_idx > 0, 1.0, 0.0)
    row = lax.broadcasted_iota(jnp.int32, (c, c), 0)
    col = lax.broadcasted_iota(jnp.int32, (c, c), 1)
    causal = row >= col
    tril = jnp.where(causal, 1.0, 0.0).astype(BF16)
    lane = lax.broadcasted_iota(jnp.int32, (c, LANE), 1)
    first_head = lane < hp
    expand = expand_ref[...]

    for r0 in range(0, cur_ref.shape[0], c):
        rows = slice(r0, r0 + c)
        small = cur_ref[rows, small_lo:small_lo + LANE]
        dt = _softplus(small + dtb_ref[...])
        acs = _dot_sel_lhs(tril, dt * -jnp.exp(alog_ref[...]))
        acs_t = acs.T
        dt_w = _dot_sel_rhs(dt, expand)
        acs_w = _dot_sel_rhs(acs, expand)
        e_acs_w = jnp.exp(acs_w)
        e_rest_w = jnp.exp(acs_w[c - 1:c, :] - acs_w)

        def conv(lo, width, r0=r0, rows=rows):
            cols = slice(x_lo + lo, x_lo + lo + width)
            halo = _halo_rows(cur_ref, prev_ref, r0, cols, use_prev)
            return _silu(_causal_conv(halo, cur_ref[rows, cols], cw_ref, lo) + cb_ref[:, lo:lo + width])

        for g in range(SSM_GROUPS):
            bg = conv(SSM_WIDTH + g * SSM_STATE, SSM_STATE)
            cg = conv(SSM_WIDTH + (SSM_GROUPS + g) * SSM_STATE, SSM_STATE)
            bg_b = bg.astype(BF16)
            cg_b = cg.astype(BF16)
            cb = _dot_nt(cg_b, bg_b)
            cols = slice(g * gw, (g + 1) * gw)
            xs = conv(g * gw, gw)
            xc = xs * dt_w[:, cols]
            state = st_ref[g]
            y = _dot(cg_b, state.astype(BF16)) * e_acs_w[:, cols]
            chunk_state = _dot(bg.T.astype(BF16), (xc * e_rest_w[:, cols]).astype(BF16))
            st_ref[g] = state * e_acs_w[c - 1:c, cols] + chunk_state
            pieces = []
            for pr in range(gw // LANE):
                xp = xc[:, pr * LANE:(pr + 1) * LANE]
                lhs = []
                for r in range(2):
                    hidx = g * (SSM_HEADS // SSM_GROUPS) + 2 * pr + r
                    seg = jnp.exp(jnp.where(causal, acs[:, hidx:hidx + 1] - acs_t[hidx:hidx + 1, :], -jnp.inf))
                    lhs.append((cb * seg).astype(BF16))
                rhs = jnp.concatenate([jnp.where(first_head, xp, 0.0), jnp.where(first_head, 0.0, xp)], axis=0)
                pieces.append(_dot(jnp.concatenate(lhs, axis=1), rhs.astype(BF16)))
            y = y + jnp.concatenate(pieces, axis=1) + xs * dskip_ref[:, cols]
            y = y * _silu(cur_ref[rows, z_lo + g * gw:z_lo + (g + 1) * gw])
            o_ref[rows, cols] = (_rms(y) * onw_ref[:, cols]).astype(o_ref.dtype)


def _ssd_mixer(proj, conv_w, conv_b, a_log_row, dt_bias_row, d_skip_w, out_norm_w, expand, *, chunks_per_block=2):
    b, t, width = proj.shape
    c = chunks_per_block * MIX_CHUNK
    xbc = SSM_WIDTH + 2 * SSM_GROUPS * SSM_STATE
    halo_blocks = c // CONV_HALO
    const = lambda i, j: (0, 0)
    return pl.pallas_call(
        _ssd_kernel,
        grid=(b, t // c),
        in_specs=[
            pl.BlockSpec((None, c, width), lambda i, j: (i, j, 0)),
            pl.BlockSpec((None, CONV_HALO, width), lambda i, j: (i, jnp.maximum(j * halo_blocks - 1, 0), 0)),
            pl.BlockSpec((CONV_K, xbc), const),
            pl.BlockSpec((1, xbc), const),
            pl.BlockSpec((1, LANE), const),
            pl.BlockSpec((1, LANE), const),
            pl.BlockSpec((1, SSM_WIDTH), const),
            pl.BlockSpec((1, SSM_WIDTH), const),
            pl.BlockSpec((LANE, SSM_WIDTH), const),
        ],
        out_specs=pl.BlockSpec((None, c, SSM_WIDTH), lambda i, j: (i, j, 0)),
        out_shape=jax.ShapeDtypeStruct((b, t, SSM_WIDTH), BF16),
        scratch_shapes=[pltpu.VMEM((SSM_GROUPS, SSM_STATE, SSM_GROUP_WIDTH), F32)],
        compiler_params=_cparams(("parallel", "arbitrary")),
        name="ssd_mixer",
    )(proj, proj, conv_w, conv_b.reshape(1, xbc), a_log_row, dt_bias_row, d_skip_w, out_norm_w.reshape(1, SSM_WIDTH),
      expand)


def _out_proj_kernel(head_ref, h_ref, og_ref, os_ref, wg_ref, ws_ref, o_ref, *, head_tiles):
    mixed = _dot(og_ref[...], wg_ref[...]) + _dot(os_ref[...], ws_ref[...])

    @pl.when(pl.program_id(0) < head_tiles)
    def _():
        o_ref[...] = head_ref[...] + mixed

    @pl.when(pl.program_id(0) >= head_tiles)
    def _():
        o_ref[...] = h_ref[...] + mixed


def _out_proj(h_head, h, o_gdn, o_ssm, w, *, tm=512):
    n, d = h.shape
    head_tiles = h_head.shape[0] // tm
    assert GDN_WIDTH == SSM_WIDTH and head_tiles >= 1 and h_head.shape[0] % tm == 0
    return pl.pallas_call(
        functools.partial(_out_proj_kernel, head_tiles=head_tiles),
        grid=(n // tm,),
        in_specs=[
            pl.BlockSpec((tm, d), lambda i: (jnp.minimum(i, head_tiles - 1), 0)),
            pl.BlockSpec((tm, d), lambda i: (i, 0)),
            pl.BlockSpec((tm, GDN_WIDTH), lambda i: (i, 0)),
            pl.BlockSpec((tm, SSM_WIDTH), lambda i: (i, 0)),
            pl.BlockSpec((GDN_WIDTH, d), lambda i: (0, 0)),
            pl.BlockSpec((SSM_WIDTH, d), lambda i: (1, 0)),
        ],
        out_specs=pl.BlockSpec((tm, d), lambda i: (i, 0)),
        out_shape=jax.ShapeDtypeStruct((n, d), F32),
        compiler_params=_cparams(("parallel",)),
        name="out_proj",
    )(h_head, h, o_gdn, o_ssm, w, w)


def _ple_kernel(h_ref, hn_ref, p_ref, wg_ref, wp_ref, pnw_ref, fnw_ref, o_ref, *, slabs):
    for lo, hi in zip(slabs[:-1], slabs[1:]):
        rows = pl.ds(lo, hi - lo)
        gate = _sigmoid(_dot(hn_ref[rows, :], wg_ref[...]))
        emb = _rms(_dot(p_ref[rows, :].astype(BF16), wp_ref[...])) * pnw_ref[...]
        o_ref[rows, :] = _rms(h_ref[rows, :] + gate * emb) * fnw_ref[...]


def _ple(h, hn, p, w_gate, w_proj, post_norm_w, final_norm_w, *, tm=512, slabs=(0, 256, 512)):
    n, d = h.shape
    pd = p.shape[1]
    const = lambda i: (0, 0)
    return pl.pallas_call(
        functools.partial(_ple_kernel, slabs=slabs),
        grid=(n // tm,),
        in_specs=[
            pl.BlockSpec((tm, d), lambda i: (i, 0)),
            pl.BlockSpec((tm, d), lambda i: (i, 0)),
            pl.BlockSpec((tm, pd), lambda i: (i, 0)),
            pl.BlockSpec((d, d), const),
            pl.BlockSpec((pd, d), const),
            pl.BlockSpec((1, d), const),
            pl.BlockSpec((1, d), const),
        ],
        out_specs=pl.BlockSpec((tm, d), lambda i: (i, 0)),
        out_shape=jax.ShapeDtypeStruct((n, d), F32),
        compiler_params=_cparams(("parallel",)),
        name="ple_final",
    )(h, hn, p, w_gate, w_proj, post_norm_w.reshape(1, d), final_norm_w.reshape(1, d))


def _lane_row(v):
    return jnp.zeros((1, LANE), F32).at[0, :v.shape[0]].set(v.astype(F32))


def kernel(x, p, ffn1_norm, ffn1_w_gate, ffn1_w_up, ffn1_w_down, mix_norm, w_in, gdn_conv_w, gdn_a_log, gdn_dt_bias, gdn_out_norm, ssm_conv_w, ssm_conv_b, ssm_a_log, ssm_dt_bias, ssm_d, ssm_out_norm, w_out, ffn2_norm, ffn2_w_gate, ffn2_w_up, ffn2_w_down, ple_norm, ple_w_gate, ple_w_proj, ple_post_norm, final_norm):
    bsz, seq, d = x.shape
    n = bsz * seq
    depth = p.shape[0]
    o_z = 4 * GDN_WIDTH + 2 * GDN_HEADS
    gdn_cols = 4 * GDN_WIDTH + LANE
    expand = (jnp.arange(LANE)[:, None] == (jnp.arange(SSM_WIDTH)[None, :] // SSM_HEAD_DIM)).astype(BF16)
    ffn_tm, ffn1_tf, ffn2_tf = 1024, 256, 512

    h = x.reshape(n, d)
    for i in range(depth):
        h1, hn1, wg1, wu1, wd1 = _ffn(
            h, ffn1_norm[i], ffn1_w_gate[i], ffn1_w_up[i], ffn1_w_down[i], mix_norm[i],
            tm=ffn_tm, tf=ffn1_tf, x_single_buffer=True, first_tile_only=True)
        wi_t = jnp.swapaxes(w_in[i], 0, 1)
        jobs = _cast_jobs((n // ffn_tm, ffn1_w_gate.shape[2] // ffn1_tf),
                          [ffn2_w_gate[i], ffn2_w_up[i], ffn2_w_down[i], w_out[i], ple_w_gate[i], wi_t])
        h, hn, wg2, wu2, wd2, wo, wpg, wi_t = _ffn(
            h, ffn1_norm[i], wg1, wu1, wd1, mix_norm[i], jobs, tm=ffn_tm, tf=ffn1_tf, skip_first_tile=True)
        pad = jnp.zeros((LANE - SSM_HEADS, d), BF16)
        w_ssd_t = jnp.concatenate([wi_t[o_z:], pad], axis=0)
        proj_gdn = _matmul_nt(hn1, hn, wi_t, cols=gdn_cols, tn=gdn_cols // 3)
        proj_ssd = _matmul_nt(hn1, hn, w_ssd_t, tn=w_ssd_t.shape[0] // 3)
        o_gdn = _gdn_mixer(proj_gdn.reshape(bsz, seq, -1), gdn_conv_w[i], _lane_row(gdn_a_log[i]),
                           _lane_row(gdn_dt_bias[i]), gdn_out_norm[i])
        o_ssm = _ssd_mixer(proj_ssd.reshape(bsz, seq, -1), ssm_conv_w[i], ssm_conv_b[i],
                           _lane_row(ssm_a_log[i]), _lane_row(ssm_dt_bias[i]),
                           jnp.repeat(ssm_d[i].astype(F32), SSM_HEAD_DIM).reshape(1, SSM_WIDTH),
                           ssm_out_norm[i], expand)
        h = _out_proj(h1, h, o_gdn.reshape(n, GDN_WIDTH), o_ssm.reshape(n, SSM_WIDTH), wo)
        h, hn = _ffn(h, ffn2_norm[i], wg2, wu2, wd2, ple_norm[i], tm=ffn_tm, tf=ffn2_tf, on_single_buffer=True)
        assert i == depth - 1, "depth > 1 would need an un-normalised variant of the embedding kernel"
        h = _ple(h, hn, p[i].reshape(n, -1), wpg, ple_w_proj[i].astype(BF16), ple_post_norm[i], final_norm)
    return h.reshape(bsz, seq, d)
```

```python
import functools

import jax
import jax.numpy as jnp
from jax import lax
from jax.experimental import pallas as pl
from jax.experimental.pallas import tpu as pltpu

F32 = jnp.float32
BF16 = jnp.bfloat16
EPS = 1e-6

LANE = 128
CONV_K = 4
CONV_HALO = 8
MIX_CHUNK = 128
GDN_HEADS = 8
GDN_HEAD_DIM = 128
GDN_WIDTH = GDN_HEADS * GDN_HEAD_DIM
GDN_HEAD_GROUP = 8
SSM_HEADS = 16
SSM_HEAD_DIM = 64
SSM_WIDTH = SSM_HEADS * SSM_HEAD_DIM
SSM_GROUPS = 2
SSM_STATE = 128
SSM_GROUP_WIDTH = SSM_WIDTH // SSM_GROUPS
VMEM_LIMIT = 60 * 1024 * 1024


def _cparams(sem):
    return pltpu.CompilerParams(dimension_semantics=sem, vmem_limit_bytes=VMEM_LIMIT)


def _dot(a, b):
    return jnp.dot(a, b, preferred_element_type=F32)


def _dot_nt(a, b):
    return lax.dot_general(a, b, (((1,), (1,)), ((), ())), preferred_element_type=F32)


def _bdot(a, b):
    return _dot(a.astype(BF16), b.astype(BF16))


def _split3(a):
    a1 = a.astype(BF16)
    r1 = a - a1.astype(F32)
    a2 = r1.astype(BF16)
    a3 = (r1 - a2.astype(F32)).astype(BF16)
    return a1, a2, a3


def _dot_sel_lhs(sel, b):
    b1, b2, b3 = _split3(b)
    return _dot(sel, b1) + (_dot(sel, b2) + _dot(sel, b3))


def _dot_sel_rhs(a, sel):
    a1, a2, a3 = _split3(a)
    return _dot(a1, sel) + (_dot(a2, sel) + _dot(a3, sel))


def _sigmoid(x):
    return 1.0 / (1.0 + jnp.exp(-x))


def _silu(x):
    return x * _sigmoid(x)


def _softplus(x):
    return jnp.maximum(x, 0.0) + jnp.log1p(jnp.exp(-jnp.abs(x)))


def _rms(x):
    return x * lax.rsqrt(jnp.mean(x * x, axis=-1, keepdims=True) + EPS)


def _ffn_kernel(*refs, n_cast, slabs, emit_weights, skip_first_tile):
    it = iter(refs)
    x_ref, nw_ref, wg_ref, wu_ref, wd_ref, nnw_ref = (next(it) for _ in range(6))
    cast_in = [next(it) for _ in range(n_cast)]
    o_ref, on_ref = next(it), next(it)
    wb_refs = [next(it) for _ in range(3)] if emit_weights else ()
    cast_out = [next(it) for _ in range(n_cast)]
    xn_ref = next(it)
    j = pl.program_id(1)
    last = pl.num_programs(1) - 1
    tm = x_ref.shape[0]
    slab_rows = [pl.ds(r * (tm // slabs), tm // slabs) for r in range(slabs)]

    def active(cond):
        return cond & (pl.program_id(0) >= 1) if skip_first_tile else cond

    if skip_first_tile:
        @pl.when((pl.program_id(0) == 0) & (j == 0))
        def _():
            o_ref[...] = jnp.zeros_like(o_ref)
            on_ref[...] = jnp.zeros_like(on_ref)

    def weights():
        if not emit_weights:
            return wg_ref[...], wu_ref[...], wd_ref[...]
        ws = [r[...].astype(BF16) for r in (wg_ref, wu_ref, wd_ref)]
        for dst, w in zip(wb_refs, ws):
            dst[...] = w
        return ws

    for src, dst in zip(cast_in, cast_out):
        dst[...] = src[...].astype(BF16)

    def step(rows, first, final, w):
        wg, wu, wd = w
        if first:
            base = x_ref[rows, :]
            xn = (_rms(base) * nw_ref[...]).astype(BF16)
            xn_ref[rows, :] = xn
        else:
            base = o_ref[rows, :]
            xn = xn_ref[rows, :]
        g = _dot(xn, wg)
        u = _dot(xn, wu)
        h = (0.5 * _silu(g) * u).astype(BF16)
        out = base + _dot(h, wd)
        o_ref[rows, :] = out
        if final:
            on_ref[rows, :] = (_rms(out) * nnw_ref[...]).astype(BF16)

    @pl.when(active(j == 0))
    def _():
        w = weights()
        for rows in slab_rows:
            step(rows, True, False, w)

    @pl.when(active((j > 0) & (j < last)))
    def _():
        step(slice(None), False, False, weights())

    @pl.when(active(j == last))
    def _():
        w = weights()
        for rows in slab_rows:
            step(rows, False, True, w)


def _ffn(x, norm_w, wg, wu, wd, next_norm_w, cast_jobs=(), *, tm, tf, slabs=2, x_single_buffer=False,
         on_single_buffer=False, first_tile_only=False, skip_first_tile=False):
    n, d = x.shape
    f = wg.shape[1]
    row_tiles = 1 if first_tile_only else n // tm
    grid = (row_tiles, f // tf)
    assert grid[1] >= 2 and not (first_tile_only and (cast_jobs or skip_first_tile))
    cast_specs = [pl.BlockSpec(blk, imap) for _, blk, imap in cast_jobs]
    vec_spec = pl.BlockSpec((1, d), lambda i, j: (0, 0))
    row_spec = pl.BlockSpec((tm, d), lambda i, j: (i, 0))
    if skip_first_tile:
        w_col = lambda i, j: (0, jnp.where(i == 0, 0, j))
        w_row = lambda i, j: (jnp.where(i == 0, 0, j), 0)
    else:
        w_col = lambda i, j: (0, j)
        w_row = lambda i, j: (j, 0)
    x_spec = pl.BlockSpec((tm, d), lambda i, j: (i, 0), pipeline_mode=pl.Buffered(1)) if x_single_buffer else row_spec
    on_spec = pl.BlockSpec((tm, d), lambda i, j: (i, 0), pipeline_mode=pl.Buffered(1)) if on_single_buffer else row_spec
    w_specs = [pl.BlockSpec((d, tf), w_col), pl.BlockSpec((d, tf), w_col), pl.BlockSpec((tf, d), w_row)]
    return pl.pallas_call(
        functools.partial(_ffn_kernel, n_cast=len(cast_jobs), slabs=slabs, emit_weights=first_tile_only,
                          skip_first_tile=skip_first_tile),
        grid=grid,
        in_specs=[x_spec, vec_spec] + w_specs + [vec_spec] + cast_specs,
        out_specs=[row_spec, on_spec] + (w_specs if first_tile_only else []) + cast_specs,
        out_shape=[jax.ShapeDtypeStruct((row_tiles * tm, d), F32), jax.ShapeDtypeStruct((row_tiles * tm, d), BF16)]
        + ([jax.ShapeDtypeStruct(w.shape, BF16) for w in (wg, wu, wd)] if first_tile_only else [])
        + [jax.ShapeDtypeStruct(job[0].shape, BF16) for job in cast_jobs],
        scratch_shapes=[pltpu.VMEM((tm, d), BF16)],
        compiler_params=_cparams(("parallel", "arbitrary")),
        name="ffn",
    )(x, norm_w.reshape(1, d), wg, wu, wd, next_norm_w.reshape(1, d), *[job[0] for job in cast_jobs])


def _cast_jobs(grid, mats):
    gi, gj = grid
    jobs = []
    for a in mats:
        r, c = a.shape
        if r % (gi * 8) == 0 and c % (gj * LANE) == 0:
            jobs.append((a, (r // gi, c // gj), lambda i, j: (i, j)))
        elif r % (gj * 8) == 0 and c % (gi * LANE) == 0:
            jobs.append((a, (r // gj, c // gi), lambda i, j: (j, i)))
        else:
            nb = max(k for k in range(1, gj + 1) if c % (k * LANE) == 0)
            assert r % (gi * 8) == 0
            jobs.append((a, (r // gi, c // nb), lambda i, j, nb=nb: (i, jnp.minimum(j, nb - 1))))
    return jobs


def _matmul_nt_kernel(head_ref, x_ref, w_ref, o_ref, *, head_tiles):
    @pl.when(pl.program_id(1) < head_tiles)
    def _():
        o_ref[...] = _dot_nt(head_ref[...], w_ref[...])

    @pl.when(pl.program_id(1) >= head_tiles)
    def _():
        o_ref[...] = _dot_nt(x_ref[...], w_ref[...])


def _matmul_nt(x_head, xn, wt, *, cols=None, tm=1024, tn):
    n, d = xn.shape
    cols = wt.shape[0] if cols is None else cols
    head_tiles = x_head.shape[0] // tm
    assert cols % tn == 0 and cols <= wt.shape[0] and head_tiles >= 1 and x_head.shape[0] % tm == 0
    return pl.pallas_call(
        functools.partial(_matmul_nt_kernel, head_tiles=head_tiles),
        grid=(cols // tn, n // tm),
        in_specs=[
            pl.BlockSpec((tm, d), lambda j, i: (jnp.minimum(i, head_tiles - 1), 0)),
            pl.BlockSpec((tm, d), lambda j, i: (i, 0)),
            pl.BlockSpec((tn, d), lambda j, i: (j, 0)),
        ],
        out_specs=pl.BlockSpec((tm, tn), lambda j, i: (i, j)),
        out_shape=jax.ShapeDtypeStruct((n, cols), F32),
        compiler_params=_cparams(("parallel", "parallel")),
        name="in_proj",
    )(x_head, xn, wt)


def _causal_conv(halo, cur, w_ref, w_lo):
    c, width = cur.shape
    xx = jnp.concatenate([halo, cur], axis=0)
    y = w_ref[CONV_K - 1:CONV_K, w_lo:w_lo + width] * cur
    for j in range(CONV_K - 1):
        shifted = pltpu.roll(xx, CONV_K - 1 - j, axis=0)
        y = y + w_ref[j:j + 1, w_lo:w_lo + width] * shifted[CONV_HALO:CONV_HALO + c]
    return y


def _halo_rows(cur_ref, prev_ref, r0, cols, use_prev):
    if r0 == 0:
        return prev_ref[:, cols] * use_prev
    return cur_ref[r0 - CONV_HALO:r0, cols]


def _inv_unit_lower_many(mats, row, col):
    c = mats[0].shape[0]
    base = 16
    diag_blk = (row // base) == (col // base)
    eye = jnp.where(row == col, 1.0, 0.0)
    ps = [jnp.where(diag_blk, -a, 0.0) for a in mats]
    ts = [eye + p for p in ps]
    for _ in range(3):
        ps = [_bdot(p, p) for p in ps]
        ts = [t + _bdot(t, p) for t, p in zip(ts, ps)]
    b = 2 * base
    while b <= c:
        off = ((row // b) == (col // b)) & ((row // (b // 2)) != (col // (b // 2)))
        ets = [_bdot(jnp.where(off, a, 0.0), t) for a, t in zip(mats, ts)]
        ts = [t - _bdot(t, et) for t, et in zip(ts, ets)]
        b *= 2
    return ts


def _gdn_kernel(cur_ref, prev_ref, cw_ref, alog_ref, dtb_ref, onw_ref, o_ref, s_ref):
    t_idx = pl.program_id(1)

    @pl.when(t_idx == 0)
    def _():
        s_ref[...] = jnp.zeros_like(s_ref)

    c = MIX_CHUNK
    hd = GDN_HEAD_DIM
    use_prev = jnp.where(t_idx > 0, 1.0, 0.0)
    row = lax.broadcasted_iota(jnp.int32, (c, c), 0)
    col = lax.broadcasted_iota(jnp.int32, (c, c), 1)
    for r0 in range(0, cur_ref.shape[0], c):
        _gdn_chunk(r0, cur_ref, prev_ref, cw_ref, alog_ref, dtb_ref, onw_ref, o_ref, s_ref, use_prev, row, col)


def _gdn_chunk(r0, cur_ref, prev_ref, cw_ref, alog_ref, dtb_ref, onw_ref, o_ref, s_ref, use_prev, row, col):
    c = MIX_CHUNK
    hd = GDN_HEAD_DIM
    rows = slice(r0, r0 + c)
    causal = row >= col
    strict = row > col
    tril = jnp.where(causal, 1.0, 0.0).astype(BF16)

    small = cur_ref[rows, 4 * GDN_WIDTH:4 * GDN_WIDTH + LANE]
    g = -jnp.exp(alog_ref[...]) * _softplus(small + dtb_ref[...])
    beta = _sigmoid(small)
    gc = _dot_sel_lhs(tril, g)
    gc_t = gc.T
    g_last = gc[c - 1:c, :]
    e_gc = jnp.exp(gc)
    e_rest = jnp.exp(g_last - gc)
    g_tot = jnp.exp(g_last)

    def conv(lo):
        cols = slice(lo, lo + hd)
        halo = _halo_rows(cur_ref, prev_ref, r0, cols, use_prev)
        return _silu(_causal_conv(halo, cur_ref[rows, cols], cw_ref, lo))

    def l2norm(x):
        return x * lax.rsqrt(jnp.sum(x * x, axis=-1, keepdims=True) + EPS)

    for g0 in range(0, GDN_HEADS, GDN_HEAD_GROUP):
        heads = range(g0, g0 + GDN_HEAD_GROUP)
        q = {h: l2norm(conv(h * hd)) * (hd ** -0.5) for h in heads}
        k = {h: l2norm(conv(GDN_WIDTH + h * hd)) for h in heads}
        v = {h: conv(2 * GDN_WIDTH + h * hd) for h in heads}
        beta_c = {h: beta[:, GDN_HEADS + h:GDN_HEADS + h + 1] for h in heads}
        decay = {h: jnp.exp(jnp.where(causal, gc[:, h:h + 1] - gc_t[h:h + 1, :], -jnp.inf)) for h in heads}
        kb = {h: k[h] * beta_c[h] for h in heads}
        k_b = {h: k[h].astype(BF16) for h in heads}
        a = {h: jnp.where(strict, _dot_nt(kb[h].astype(BF16), k_b[h]) * decay[h], 0.0) for h in heads}
        qk = {h: (_dot_nt(q[h].astype(BF16), k_b[h]) * decay[h]).astype(BF16) for h in heads}
        t_inv = dict(zip(heads, _inv_unit_lower_many([a[h] for h in heads], row, col)))
        uw = {h: _bdot(t_inv[h], jnp.concatenate([v[h] * beta_c[h], kb[h] * e_gc[:, h:h + 1]], axis=1))
              for h in heads}
        s = {h: s_ref[h] for h in heads}
        wq = {h: jnp.concatenate([uw[h][:, hd:], q[h] * e_gc[:, h:h + 1]], axis=0) for h in heads}
        wq_s = {h: _bdot(wq[h], s[h]) for h in heads}
        v_new = {h: (uw[h][:, :hd] - wq_s[h][:c]).astype(BF16) for h in heads}
        o = {h: wq_s[h][c:] + _dot(qk[h], v_new[h]) for h in heads}
        k_dec_t = {h: (k[h] * e_rest[:, h:h + 1]).T.astype(BF16) for h in heads}
        for h in heads:
            s_ref[h] = s[h] * g_tot[:, h:h + 1] + _dot(k_dec_t[h], v_new[h])
        for h in heads:
            gate = cur_ref[rows, 3 * GDN_WIDTH + h * hd:3 * GDN_WIDTH + (h + 1) * hd]
            o_ref[rows, h * hd:(h + 1) * hd] = (_rms(o[h]) * onw_ref[...] * _silu(gate)).astype(o_ref.dtype)


def _gdn_mixer(proj, conv_w, a_log_row, dt_bias_row, out_norm_w, *, chunks_per_block=2):
    b, t, width = proj.shape
    c = chunks_per_block * MIX_CHUNK
    qkv = 3 * GDN_WIDTH
    halo_blocks = c // CONV_HALO
    return pl.pallas_call(
        _gdn_kernel,
        grid=(b, t // c),
        in_specs=[
            pl.BlockSpec((None, c, width), lambda i, j: (i, j, 0)),
            pl.BlockSpec((None, CONV_HALO, qkv), lambda i, j: (i, jnp.maximum(j * halo_blocks - 1, 0), 0)),
            pl.BlockSpec((CONV_K, qkv), lambda i, j: (0, 0)),
            pl.BlockSpec((1, LANE), lambda i, j: (0, 0)),
            pl.BlockSpec((1, LANE), lambda i, j: (0, 0)),
            pl.BlockSpec((1, GDN_HEAD_DIM), lambda i, j: (0, 0)),
        ],
        out_specs=pl.BlockSpec((None, c, GDN_WIDTH), lambda i, j: (i, j, 0)),
        out_shape=jax.ShapeDtypeStruct((b, t, GDN_WIDTH), BF16),
        scratch_shapes=[pltpu.VMEM((GDN_HEADS, GDN_HEAD_DIM, GDN_HEAD_DIM), F32)],
        compiler_params=_cparams(("parallel", "arbitrary")),
        name="gdn_mixer",
    )(proj, proj, conv_w, a_log_row, dt_bias_row, out_norm_w.reshape(1, GDN_HEAD_DIM))


def _ssd_kernel(cur_ref, prev_ref, cw_ref, cb_ref, alog_ref, dtb_ref, dskip_ref, onw_ref, expand_ref,
                o_ref, st_ref):
    t_idx = pl.program_id(1)

    @pl.when(t_idx == 0)
    def _():
        st_ref[...] = jnp.zeros_like(st_ref)

    c = MIX_CHUNK
    hp = SSM_HEAD_DIM
    gw = SSM_GROUP_WIDTH
    z_lo = 0
    x_lo = SSM_WIDTH
    small_lo = x_lo + SSM_WIDTH + 2 * SSM_GROUPS * SSM_STATE
    use_prev = jnp.where(t_idx > 0, 1.0, 0.0)
    row = lax.broadcasted_iota(jnp.int32, (c, c), 0)
    col = lax.broadcasted_iota(jnp.int32, (c, c), 1)
    causal = row >= col
    tril = jnp.where(causal, 1.0, 0.0).astype(BF16)
    lane = lax.broadcasted_iota(jnp.int32, (c, LANE), 1)
    first_head = lane < hp
    expand = expand_ref[...]

    for r0 in range(0, cur_ref.shape[0], c):
        rows = slice(r0, r0 + c)
        small = cur_ref[rows, small_lo:small_lo + LANE]
        dt = _softplus(small + dtb_ref[...])
        acs = _dot_sel_lhs(tril, dt * -jnp.exp(alog_ref[...]))
        acs_t = acs.T
        dt_w = _dot_sel_rhs(dt, expand)
        acs_w = _dot_sel_rhs(acs, expand)
        e_acs_w = jnp.exp(acs_w)
        e_rest_w = jnp.exp(acs_w[c - 1:c, :] - acs_w)

        def conv(lo, width, r0=r0, rows=rows):
            cols = slice(x_lo + lo, x_lo + lo + width)
            halo = _halo_rows(cur_ref, prev_ref, r0, cols, use_prev)
            return _silu(_causal_conv(halo, cur_ref[rows, cols], cw_ref, lo) + cb_ref[:, lo:lo + width])

        for g in range(SSM_GROUPS):
            bg = conv(SSM_WIDTH + g * SSM_STATE, SSM_STATE)
            cg = conv(SSM_WIDTH + (SSM_GROUPS + g) * SSM_STATE, SSM_STATE)
            bg_b = bg.astype(BF16)
            cg_b = cg.astype(BF16)
            cb = _dot_nt(cg_b, bg_b)
            cols = slice(g * gw, (g + 1) * gw)
            xs = conv(g * gw, gw)
            xc = xs * dt_w[:, cols]
            state = st_ref[g]
            y = _dot(cg_b, state.astype(BF16)) * e_acs_w[:, cols]
            chunk_state = _dot(bg.T.astype(BF16), (xc * e_rest_w[:, cols]).astype(BF16))
            st_ref[g] = state * e_acs_w[c - 1:c, cols] + chunk_state
            pieces = []
            for pr in range(gw // LANE):
                xp = xc[:, pr * LANE:(pr + 1) * LANE]
                lhs = []
                for r in range(2):
                    hidx = g * (SSM_HEADS // SSM_GROUPS) + 2 * pr + r
                    seg = jnp.exp(jnp.where(causal, acs[:, hidx:hidx + 1] - acs_t[hidx:hidx + 1, :], -jnp.inf))
                    lhs.append((cb * seg).astype(BF16))
                rhs = jnp.concatenate([jnp.where(first_head, xp, 0.0), jnp.where(first_head, 0.0, xp)], axis=0)
                pieces.append(_dot(jnp.concatenate(lhs, axis=1), rhs.astype(BF16)))
            y = y + jnp.concatenate(pieces, axis=1) + xs * dskip_ref[:, cols]
            y = y * _silu(cur_ref[rows, z_lo + g * gw:z_lo + (g + 1) * gw])
            o_ref[rows, cols] = (_rms(y) * onw_ref[:, cols]).astype(o_ref.dtype)


def _ssd_mixer(proj, conv_w, conv_b, a_log_row, dt_bias_row, d_skip_w, out_norm_w, expand, *, chunks_per_block=2):
    b, t, width = proj.shape
    c = chunks_per_block * MIX_CHUNK
    xbc = SSM_WIDTH + 2 * SSM_GROUPS * SSM_STATE
    halo_blocks = c // CONV_HALO
    const = lambda i, j: (0, 0)
    return pl.pallas_call(
        _ssd_kernel,
        grid=(b, t // c),
        in_specs=[
            pl.BlockSpec((None, c, width), lambda i, j: (i, j, 0)),
            pl.BlockSpec((None, CONV_HALO, width), lambda i, j: (i, jnp.maximum(j * halo_blocks - 1, 0), 0)),
            pl.BlockSpec((CONV_K, xbc), const),
            pl.BlockSpec((1, xbc), const),
            pl.BlockSpec((1, LANE), const),
            pl.BlockSpec((1, LANE), const),
            pl.BlockSpec((1, SSM_WIDTH), const),
            pl.BlockSpec((1, SSM_WIDTH), const),
            pl.BlockSpec((LANE, SSM_WIDTH), const),
        ],
        out_specs=pl.BlockSpec((None, c, SSM_WIDTH), lambda i, j: (i, j, 0)),
        out_shape=jax.ShapeDtypeStruct((b, t, SSM_WIDTH), BF16),
        scratch_shapes=[pltpu.VMEM((SSM_GROUPS, SSM_STATE, SSM_GROUP_WIDTH), F32)],
        compiler_params=_cparams(("parallel", "arbitrary")),
        name="ssd_mixer",
    )(proj, proj, conv_w, conv_b.reshape(1, xbc), a_log_row, dt_bias_row, d_skip_w, out_norm_w.reshape(1, SSM_WIDTH),
      expand)


def _out_proj_kernel(head_ref, h_ref, og_ref, os_ref, wg_ref, ws_ref, o_ref, *, head_tiles):
    def run(res_ref):
        o_ref[...] = res_ref[...] + _dot(og_ref[...], wg_ref[...]) + _dot(os_ref[...], ws_ref[...])

    @pl.when(pl.program_id(0) < head_tiles)
    def _():
        run(head_ref)

    @pl.when(pl.program_id(0) >= head_tiles)
    def _():
        run(h_ref)


def _out_proj(h_head, h, o_gdn, o_ssm, w, *, tm=512):
    n, d = h.shape
    head_tiles = h_head.shape[0] // tm
    assert GDN_WIDTH == SSM_WIDTH and head_tiles >= 1 and h_head.shape[0] % tm == 0
    return pl.pallas_call(
        functools.partial(_out_proj_kernel, head_tiles=head_tiles),
        grid=(n // tm,),
        in_specs=[
            pl.BlockSpec((tm, d), lambda i: (jnp.minimum(i, head_tiles - 1), 0)),
            pl.BlockSpec((tm, d), lambda i: (i, 0)),
            pl.BlockSpec((tm, GDN_WIDTH), lambda i: (i, 0)),
            pl.BlockSpec((tm, SSM_WIDTH), lambda i: (i, 0)),
            pl.BlockSpec((GDN_WIDTH, d), lambda i: (0, 0)),
            pl.BlockSpec((SSM_WIDTH, d), lambda i: (1, 0)),
        ],
        out_specs=pl.BlockSpec((tm, d), lambda i: (i, 0)),
        out_shape=jax.ShapeDtypeStruct((n, d), F32),
        compiler_params=_cparams(("parallel",)),
        name="out_proj",
    )(h_head, h, o_gdn, o_ssm, w, w)


def _ple_kernel(h_ref, hn_ref, p_ref, wg_ref, wp_ref, pnw_ref, fnw_ref, o_ref, *, slabs):
    for lo, hi in zip(slabs[:-1], slabs[1:]):
        rows = pl.ds(lo, hi - lo)
        gate = _sigmoid(_dot(hn_ref[rows, :], wg_ref[...]))
        emb = _rms(_dot(p_ref[rows, :].astype(BF16), wp_ref[...])) * pnw_ref[...]
        o_ref[rows, :] = _rms(h_ref[rows, :] + gate * emb) * fnw_ref[...]


def _ple(h, hn, p, w_gate, w_proj, post_norm_w, final_norm_w, *, tm=512, slabs=(0, 256, 512)):
    n, d = h.shape
    pd = p.shape[1]
    const = lambda i: (0, 0)
    return pl.pallas_call(
        functools.partial(_ple_kernel, slabs=slabs),
        grid=(n // tm,),
        in_specs=[
            pl.BlockSpec((tm, d), lambda i: (i, 0)),
            pl.BlockSpec((tm, d), lambda i: (i, 0)),
            pl.BlockSpec((tm, pd), lambda i: (i, 0)),
            pl.BlockSpec((d, d), const),
            pl.BlockSpec((pd, d), const),
            pl.BlockSpec((1, d), const),
            pl.BlockSpec((1, d), const),
        ],
        out_specs=pl.BlockSpec((tm, d), lambda i: (i, 0)),
        out_shape=jax.ShapeDtypeStruct((n, d), F32),
        compiler_params=_cparams(("parallel",)),
        name="ple_final",
    )(h, hn, p, w_gate, w_proj, post_norm_w.reshape(1, d), final_norm_w.reshape(1, d))


def _lane_row(v):
    return jnp.zeros((1, LANE), F32).at[0, :v.shape[0]].set(v.astype(F32))


def kernel(x, p, ffn1_norm, ffn1_w_gate, ffn1_w_up, ffn1_w_down, mix_norm, w_in, gdn_conv_w, gdn_a_log, gdn_dt_bias, gdn_out_norm, ssm_conv_w, ssm_conv_b, ssm_a_log, ssm_dt_bias, ssm_d, ssm_out_norm, w_out, ffn2_norm, ffn2_w_gate, ffn2_w_up, ffn2_w_down, ple_norm, ple_w_gate, ple_w_proj, ple_post_norm, final_norm):
    bsz, seq, d = x.shape
    n = bsz * seq
    depth = p.shape[0]
    o_z = 4 * GDN_WIDTH + 2 * GDN_HEADS
    gdn_cols = 4 * GDN_WIDTH + LANE
    expand = (jnp.arange(LANE)[:, None] == (jnp.arange(SSM_WIDTH)[None, :] // SSM_HEAD_DIM)).astype(BF16)
    ffn_tm, ffn1_tf, ffn2_tf = 1024, 256, 512

    h = x.reshape(n, d)
    for i in range(depth):
        h1, hn1, wg1, wu1, wd1 = _ffn(
            h, ffn1_norm[i], ffn1_w_gate[i], ffn1_w_up[i], ffn1_w_down[i], mix_norm[i],
            tm=ffn_tm, tf=ffn1_tf, x_single_buffer=True, first_tile_only=True)
        wi_t = jnp.swapaxes(w_in[i], 0, 1)
        jobs = _cast_jobs((n // ffn_tm, ffn1_w_gate.shape[2] // ffn1_tf),
                          [ffn2_w_gate[i], ffn2_w_up[i], ffn2_w_down[i], w_out[i], ple_w_gate[i], wi_t])
        h, hn, wg2, wu2, wd2, wo, wpg, wi_t = _ffn(
            h, ffn1_norm[i], wg1, wu1, wd1, mix_norm[i], jobs, tm=ffn_tm, tf=ffn1_tf, skip_first_tile=True)
        pad = jnp.zeros((LANE - SSM_HEADS, d), BF16)
        w_ssd_t = jnp.concatenate([wi_t[o_z:], pad], axis=0)
        proj_gdn = _matmul_nt(hn1, hn, wi_t, cols=gdn_cols, tn=gdn_cols // 3)
        proj_ssd = _matmul_nt(hn1, hn, w_ssd_t, tn=w_ssd_t.shape[0] // 3)
        o_gdn = _gdn_mixer(proj_gdn.reshape(bsz, seq, -1), gdn_conv_w[i], _lane_row(gdn_a_log[i]),
                           _lane_row(gdn_dt_bias[i]), gdn_out_norm[i])
        o_ssm = _ssd_mixer(proj_ssd.reshape(bsz, seq, -1), ssm_conv_w[i], ssm_conv_b[i],
                           _lane_row(ssm_a_log[i]), _lane_row(ssm_dt_bias[i]),
                           jnp.repeat(ssm_d[i].astype(F32), SSM_HEAD_DIM).reshape(1, SSM_WIDTH),
                           ssm_out_norm[i], expand)
        h = _out_proj(h1, h, o_gdn.reshape(n, GDN_WIDTH), o_ssm.reshape(n, SSM_WIDTH), wo)
        h, hn = _ffn(h, ffn2_norm[i], wg2, wu2, wd2, ple_norm[i], tm=ffn_tm, tf=ffn2_tf, on_single_buffer=True)
        assert i == depth - 1, "depth > 1 would need an un-normalised variant of the embedding kernel"
        h = _ple(h, hn, p[i].reshape(n, -1), wpg, ple_w_proj[i].astype(BF16), ple_post_norm[i], final_norm)
    return h.reshape(bsz, seq, d)
```
